```python
import jax
import jax.numpy as jnp
from jax import lax
import numpy as np

D_MODEL = 1024
BATCH = 8
SEQ = 4096
DEPTH = 2
DEC_BATCH = 128
DEC_SEQ = 8
PAST_LEN = 16384
PAGE_SIZE = 128

HEAD_DIM = 64
ROT_DIM = HEAD_DIM // 4
ROPE_THETA = 500000.0
NORM_EPS = 1e-6
D_FF = 2816
N_EVEN = (DEPTH + 1) // 2
N_ODD = DEPTH // 2
DENSE_Q_BLOCK = 128
SPARSE_Q_BLOCK = 32
SAMPLE_Q_BLOCK = 1
NEG_BIG = -1e30
NSA_HEADS = D_MODEL // (2 * HEAD_DIM)
NSA_KV_HEADS = 2
NSA_GROUP = NSA_HEADS // NSA_KV_HEADS
CMP_BLOCK = 32
SEL_BLOCK = 64
N_SEL = 16
WINDOW = 512
CMP_HIDDEN = 128
FORCE_BONUS = float(NSA_GROUP + 1)
DSA_HEADS = D_MODEL // (2 * HEAD_DIM)
DSA_KV_HEADS = 2
DSA_GROUP = DSA_HEADS // DSA_KV_HEADS
IDX_HEADS = 4
IDX_DIM = 64
DSA_TOPK = 256
MLA_HEADS = 16
MLA_NOPE = 64
MLA_ROPE = 32
MLA_V = 64
MLA_QK = MLA_NOPE + MLA_ROPE
Q_LORA = 256
KV_LORA = 256

EVEN_SPLITS = (NSA_HEADS * HEAD_DIM, 3 * 2 * NSA_KV_HEADS * HEAD_DIM, 3 * NSA_HEADS, DSA_HEADS * HEAD_DIM, 2 * DSA_KV_HEADS * HEAD_DIM, IDX_HEADS * IDX_DIM, IDX_HEADS, IDX_DIM)
EVEN_IN = sum(EVEN_SPLITS)
EVEN_MIX = (NSA_HEADS + DSA_HEADS) * HEAD_DIM
ODD_IN = Q_LORA + KV_LORA + MLA_ROPE
ODD_MIX = MLA_HEADS * MLA_V

kernel_name = 'nsa_dsa_mla_macaron_decode_step'


def rms_norm(x, g):
    xf = x.astype(jnp.float32)
    y = xf * lax.rsqrt(jnp.mean(xf * xf, axis=-1, keepdims=True) + NORM_EPS)
    return (y * g.astype(jnp.float32)).astype(x.dtype)


def rope(x, pos, rot):
    half = rot // 2
    inv = ROPE_THETA ** (-jnp.arange(half, dtype=jnp.float32) / half)
    ang = pos.astype(jnp.float32)[:, None] * inv[None, :]
    shape = (1, pos.shape[0]) + (1,) * (x.ndim - 3) + (half,)
    cos = jnp.cos(ang).reshape(shape)
    sin = jnp.sin(ang).reshape(shape)
    x1 = x[..., :half].astype(jnp.float32)
    x2 = x[..., half:rot].astype(jnp.float32)
    out = jnp.concatenate([x1 * cos - x2 * sin, x2 * cos + x1 * sin], axis=-1).astype(x.dtype)
    return jnp.concatenate([out, x[..., rot:]], axis=-1)


def masked_softmax(s, mask):
    s = jnp.where(mask, s.astype(jnp.float32), -jnp.inf)
    m = jnp.max(s, axis=-1, keepdims=True)
    m = jnp.where(jnp.isfinite(m), m, 0.0)
    e = jnp.where(mask, jnp.exp(s - m), 0.0)
    return e / jnp.maximum(jnp.sum(e, axis=-1, keepdims=True), 1e-30)


def macaron_half(x, g, w_gu, w_down):
    a, u = jnp.split(rms_norm(x, g) @ w_gu, 2, axis=-1)
    return x + 0.5 * ((jax.nn.silu(a) * u) @ w_down)


def sweep(fn, n_q, block):
    o = lax.map(fn, jnp.arange(n_q // block) * block)
    return jnp.swapaxes(o, 0, 1).reshape(o.shape[1], n_q, o.shape[-1])


def gather_rows(pool, layer, new, page_table, past, b, s, *tail):
    in_past = s < past
    sp = jnp.where(in_past, s, 0)
    page = page_table[b, sp // PAGE_SIZE]
    r_old = pool[(layer, page, sp % PAGE_SIZE) + tail]
    r_new = new[(b, jnp.clip(s - past, 0, new.shape[1] - 1)) + tail]
    idx_shape = jnp.broadcast_shapes(*[a.shape for a in (b, s) + tail])
    sel = jnp.broadcast_to(in_past, idx_shape).reshape(idx_shape + (1,) * (r_old.ndim - len(idx_shape)))
    return jnp.where(sel, r_old, r_new)


def nsa_compress(rows, pe, w1, w2):
    B, N = rows.shape[:2]
    x = rows + jnp.swapaxes(pe, 0, 1)[:, :, None, :]
    x = jnp.transpose(x, (0, 1, 3, 4, 2, 5)).reshape(B, N, 2, NSA_KV_HEADS, CMP_BLOCK * HEAD_DIM)
    h = jax.nn.gelu(jnp.einsum('bnkgf,kfh->bnkgh', x, w1))
    return jnp.einsum('bnkgh,khd->bnkgd', h, w2)


def nsa_core(q, gate, q_pos, kvc, fetch_slc, kv_win, win_pos, n_keys):
    B, T = q.shape[:2]
    scale = HEAD_DIM ** -0.5
    nc = kvc.shape[1]
    c_end = (jnp.arange(nc) + 1) * CMP_BLOCK - 1
    mask_c = (c_end[None, :] <= q_pos[:, None])[None, :, None, None, :]
    p_c = masked_softmax(jnp.einsum('btgrd,bngd->btgrn', q, kvc[:, :, 0]).astype(jnp.float32) * scale, mask_c)
    o_c = jnp.einsum('btgrn,bngd->btgrd', p_c.astype(q.dtype), kvc[:, :, 1])
    ratio = SEL_BLOCK // CMP_BLOCK
    ns = -(-n_keys // SEL_BLOCK)
    imp = jnp.pad(jnp.sum(p_c, axis=3), ((0, 0), (0, 0), (0, 0), (0, ns * ratio - nc)))
    imp = imp.reshape(B, T, NSA_KV_HEADS, ns, ratio).sum(-1)
    blk = jnp.arange(ns)[None, :]
    cur = (q_pos // SEL_BLOCK)[:, None]
    valid = blk * SEL_BLOCK <= q_pos[:, None]
    forced = (blk == 0) | (blk == cur) | (blk == cur - 1)
    score = jnp.where(valid[None, :, None, :], imp + jnp.where(forced, FORCE_BONUS, 0.0)[None, :, None, :], -1.0)
    n_sel = min(N_SEL, ns)
    top_s, top_i = lax.top_k(score, n_sel)
    tok = (top_i[..., None] * SEL_BLOCK + jnp.arange(SEL_BLOCK)).reshape(B, T, NSA_KV_HEADS, n_sel * SEL_BLOCK)
    tok_ok = jnp.repeat(top_s >= 0.0, SEL_BLOCK, axis=-1) & (tok <= q_pos[None, :, None, None])
    kv_s = fetch_slc(jnp.where(tok_ok, tok, 0))
    p_s = masked_softmax(jnp.einsum('btgrd,btgkd->btgrk', q, kv_s[..., 0, :]).astype(jnp.float32) * scale, tok_ok[:, :, :, None, :])
    o_s = jnp.einsum('btgrk,btgkd->btgrd', p_s.astype(q.dtype), kv_s[..., 1, :])
    dpos = q_pos[:, None] - win_pos[None, :]
    mask_w = ((dpos >= 0) & (dpos <= WINDOW) & (win_pos[None, :] >= 0))[None, :, None, None, :]
    p_w = masked_softmax(jnp.einsum('btgrd,blgd->btgrl', q, kv_win[:, :, 0]).astype(jnp.float32) * scale, mask_w)
    o_w = jnp.einsum('btgrl,blgd->btgrd', p_w.astype(q.dtype), kv_win[:, :, 1])
    return gate[..., 0:1] * o_c + gate[..., 1:2] * o_s + gate[..., 2:3] * o_w


def dsa_core(q, qi, wi, q_pos, ki_all, fetch_kv):
    n_keys = ki_all.shape[1]
    logits = jnp.einsum('bthd,bsd->bths', qi, ki_all).astype(jnp.float32) * IDX_DIM ** -0.5
    score = jnp.einsum('bth,bths->bts', wi.astype(jnp.float32), jax.nn.relu(logits))
    adm = (jnp.arange(n_keys)[None, :] <= q_pos[:, None])[None]
    score = jnp.where(adm, score, -jnp.inf)
    top_s, top_i = lax.top_k(score, min(DSA_TOPK, n_keys // 4))
    ok = jnp.isfinite(top_s)
    kv = fetch_kv(jnp.where(ok, top_i, 0))
    p = masked_softmax(jnp.einsum('btgrd,btkgd->btgrk', q, kv[:, :, :, 0]).astype(jnp.float32) * HEAD_DIM ** -0.5, ok[:, :, None, None, :])
    return jnp.einsum('btgrk,btkgd->btgrd', p.astype(q.dtype), kv[:, :, :, 1])


def even_project(xn, w_in, nsa_g, dsa_g, pos):
    B, T, _ = xn.shape
    q_a, kv_a, gate_a, q_b, kv_b, qi_b, wi_b, ki_b = jnp.split(xn @ w_in, np.cumsum(EVEN_SPLITS)[:-1].tolist(), axis=-1)
    q_a = rope(rms_norm(q_a.reshape(B, T, NSA_KV_HEADS, NSA_GROUP, HEAD_DIM), nsa_g[0]), pos, ROT_DIM)
    kv_a = kv_a.reshape(B, T, 3, 2, NSA_KV_HEADS, HEAD_DIM)
    k_a = rope(rms_norm(kv_a[:, :, :, 0], nsa_g[1:4, None, :]), pos, ROT_DIM)
    kv_a = jnp.stack([k_a, kv_a[:, :, :, 1]], axis=3)
    gate_a = jax.nn.sigmoid(gate_a).reshape(B, T, NSA_KV_HEADS, NSA_GROUP, 3)
    q_b = rope(rms_norm(q_b.reshape(B, T, DSA_KV_HEADS, DSA_GROUP, HEAD_DIM), dsa_g[0]), pos, ROT_DIM)
    kv_b = kv_b.reshape(B, T, 2, DSA_KV_HEADS, HEAD_DIM)
    kv_b = jnp.stack([rope(rms_norm(kv_b[:, :, 0], dsa_g[1]), pos, ROT_DIM), kv_b[:, :, 1]], axis=2)
    qi_b = rope(qi_b.reshape(B, T, IDX_HEADS, IDX_DIM), pos, IDX_DIM // 4)
    ki_b = rope(ki_b, pos, IDX_DIM // 4)
    return q_a, kv_a, gate_a, q_b, kv_b, qi_b, wi_b * IDX_HEADS ** -0.5, ki_b


def even_mix_prompt(xn, w_in, nsa_g, dsa_g, pe, w1, w2):
    B, S, _ = xn.shape
    q_a, kv_a, gate_a, q_b, kv_b, qi_b, wi_b, ki_b = even_project(xn, w_in, nsa_g, dsa_g, jnp.arange(S))
    kv_cmp, kv_slc, kv_win = kv_a[:, :, 0], kv_a[:, :, 1], kv_a[:, :, 2]
    nc = S // CMP_BLOCK
    kvc = nsa_compress(kv_cmp[:, :nc * CMP_BLOCK].reshape(B, nc, CMP_BLOCK, 2, NSA_KV_HEADS, HEAD_DIM), pe, w1, w2)
    win_pad = jnp.pad(kv_win, ((0, 0), (WINDOW, 0), (0, 0), (0, 0), (0, 0)))
    b_ix = jnp.arange(B)
    g_ix = jnp.arange(NSA_KV_HEADS)[None, None, :, None, None]

    def fetch_slc(idx):
        return kv_slc[b_ix[:, None, None, None, None], idx[..., None], jnp.arange(2), g_ix]

    def fetch_dsa(idx):
        return kv_b[b_ix[:, None, None], idx]

    def block(q0):
        cut = lambda a: lax.dynamic_slice_in_dim(a, q0, SPARSE_Q_BLOCK, axis=1)
        q_pos = q0 + jnp.arange(SPARSE_Q_BLOCK)
        kv_w = lax.dynamic_slice_in_dim(win_pad, q0, WINDOW + SPARSE_Q_BLOCK, axis=1)
        w_pos = q0 - WINDOW + jnp.arange(WINDOW + SPARSE_Q_BLOCK)
        o_a = nsa_core(cut(q_a), cut(gate_a), q_pos, kvc, fetch_slc, kv_w, w_pos, S)
        o_b = dsa_core(cut(q_b), cut(qi_b), cut(wi_b), q_pos, ki_b, fetch_dsa)
        return jnp.concatenate([o_a.reshape(B, SPARSE_Q_BLOCK, -1), o_b.reshape(B, SPARSE_Q_BLOCK, -1)], axis=-1)

    o = sweep(block, S, SPARSE_Q_BLOCK)
    rows = (kv_cmp, kv_slc, kv_win[:, S - min(WINDOW, S):], kv_b, ki_b)
    return o, rows


def even_mix_sample(xn, w_in, nsa_g, dsa_g, pe, w1, w2, li, cache_cmp, cache_slc, state_win, cache_dsa, cache_idx, page_table):
    Bd, T, _ = xn.shape
    n_pages = page_table.shape[1]
    past = n_pages * PAGE_SIZE
    pos = past + jnp.arange(T)
    q_a, kv_a, gate_a, q_b, kv_b, qi_b, wi_b, ki_b = even_project(xn, w_in, nsa_g, dsa_g, pos)
    kv_cmp, kv_slc, kv_win = kv_a[:, :, 0], kv_a[:, :, 1], kv_a[:, :, 2]

    def cmp_page(pt_col):
        rows = cache_cmp[li, pt_col].reshape(Bd, PAGE_SIZE // CMP_BLOCK, CMP_BLOCK, 2, NSA_KV_HEADS, HEAD_DIM)
        return nsa_compress(rows, pe, w1, w2)

    kvc = lax.map(cmp_page, page_table.T)
    kvc = jnp.swapaxes(kvc, 0, 1).reshape(Bd, n_pages * (PAGE_SIZE // CMP_BLOCK), 2, NSA_KV_HEADS, HEAD_DIM)
    n_new = T // CMP_BLOCK
    if n_new > 0:
        kvc_new = nsa_compress(kv_cmp[:, :n_new * CMP_BLOCK].reshape(Bd, n_new, CMP_BLOCK, 2, NSA_KV_HEADS, HEAD_DIM), pe, w1, w2)
        kvc = jnp.concatenate([kvc, kvc_new], axis=1)
    b_ix = jnp.arange(Bd)
    g_ix = jnp.arange(NSA_KV_HEADS)[None, None, :, None, None]

    def fetch_slc(idx):
        return gather_rows(cache_slc, li, kv_slc, page_table, past, b_ix[:, None, None, None, None], idx[..., None], jnp.arange(2), g_ix)

    def fetch_dsa(idx):
        return gather_rows(cache_dsa, li, kv_b, page_table, past, b_ix[:, None, None], idx)

    wb = state_win.shape[2]
    kv_w = jnp.concatenate([state_win[li], kv_win], axis=1)
    w_pos = past - wb + jnp.arange(wb + T)
    ki_all = jnp.concatenate([cache_idx[li, page_table].reshape(Bd, past, IDX_DIM), ki_b], axis=1)

    def block(t0):
        cut = lambda a: lax.dynamic_slice_in_dim(a, t0, SAMPLE_Q_BLOCK, axis=1)
        q_pos = past + t0 + jnp.arange(SAMPLE_Q_BLOCK)
        o_a = nsa_core(cut(q_a), cut(gate_a), q_pos, kvc, fetch_slc, kv_w, w_pos, past + T)
        o_b = dsa_core(cut(q_b), cut(qi_b), cut(wi_b), q_pos, ki_all, fetch_dsa)
        return jnp.concatenate([o_a.reshape(Bd, SAMPLE_Q_BLOCK, -1), o_b.reshape(Bd, SAMPLE_Q_BLOCK, -1)], axis=-1)

    o = sweep(block, T, SAMPLE_Q_BLOCK)
    rows = (kv_cmp, kv_slc, kv_w[:, kv_w.shape[1] - min(WINDOW, wb + T):], kv_b, ki_b)
    return o, rows


def odd_project(xn, w_in, norm_cq, norm_ckv, w_qb, qk_g, pos):
    cq, ckv, kr = jnp.split(xn @ w_in, [Q_LORA, Q_LORA + KV_LORA], axis=-1)
    q = jnp.einsum('btr,rhd->bthd', rms_norm(cq, norm_cq), w_qb)
    q = rms_norm(jnp.concatenate([q[..., :MLA_NOPE], rope(q[..., MLA_NOPE:], pos, MLA_ROPE)], axis=-1), qk_g[0])
    latent = jnp.concatenate([rms_norm(ckv, norm_ckv), rope(kr, pos, MLA_ROPE)], axis=-1)
    return q, latent


def mla_keys(latent, w_uk, gk):
    c, kr = latent[..., :KV_LORA], latent[..., KV_LORA:]
    k_nope = jnp.einsum('bcl,lhd->bchd', c, w_uk)
    k = jnp.concatenate([k_nope, jnp.broadcast_to(kr[:, :, None, :], k_nope.shape[:3] + (MLA_ROPE,))], axis=-1)
    return rms_norm(k, gk), c


def odd_mix_prompt(xn, w_in, norm_cq, norm_ckv, w_qb, w_uk, w_uv, qk_g):
    B, S, _ = xn.shape
    q, latent = odd_project(xn, w_in, norm_cq, norm_ckv, w_qb, qk_g, jnp.arange(S))
    k, c = mla_keys(latent, w_uk, qk_g[1])
    key_pos = jnp.arange(S)

    def block(q0):
        qb = lax.dynamic_slice_in_dim(q, q0, DENSE_Q_BLOCK, axis=1)
        q_pos = q0 + jnp.arange(DENSE_Q_BLOCK)
        s = jnp.einsum('bthd,bshd->bths', qb, k).astype(jnp.float32) * MLA_QK ** -0.5
        p = masked_softmax(s, (key_pos[None, :] <= q_pos[:, None])[None, :, None, :])
        o_lat = jnp.einsum('bths,bsl->bthl', p.astype(c.dtype), c)
        return jnp.einsum('bthl,lhd->bthd', o_lat, w_uv).reshape(B, DENSE_Q_BLOCK, ODD_MIX)

    return sweep(block, S, DENSE_Q_BLOCK), latent


def odd_mix_sample(xn, w_in, norm_cq, norm_ckv, w_qb, w_uk, w_uv, qk_g, li, cache_lat, page_table):
    Bd, T, _ = xn.shape
    n_pages = page_table.shape[1]
    past = n_pages * PAGE_SIZE
    pos = past + jnp.arange(T)
    q, latent = odd_project(xn, w_in, norm_cq, norm_ckv, w_qb, qk_g, pos)

    def update(carry, lat, k_pos):
        m, l, acc = carry
        k, c = mla_keys(lat, w_uk, qk_g[1])
        s = jnp.einsum('bthd,bchd->bthc', q, k).astype(jnp.float32) * MLA_QK ** -0.5
        mask = (k_pos[None, :] <= pos[:, None])[None, :, None, :]
        s = jnp.where(mask, s, NEG_BIG)
        m_new = jnp.maximum(m, jnp.max(s, axis=-1))
        p = jnp.where(mask, jnp.exp(s - m_new[..., None]), 0.0)
        corr = jnp.exp(m - m_new)
        return (m_new, l * corr + jnp.sum(p, axis=-1), acc * corr[..., None] + jnp.einsum('bthc,bcl->bthl', p, c.astype(jnp.float32)))

    def page_step(carry, xs):
        j, pt_col = xs
        return update(carry, cache_lat[li, pt_col], j * PAGE_SIZE + jnp.arange(PAGE_SIZE)), None

    init = (jnp.full((Bd, T, MLA_HEADS), NEG_BIG, jnp.float32), jnp.zeros((Bd, T, MLA_HEADS), jnp.float32), jnp.zeros((Bd, T, MLA_HEADS, KV_LORA), jnp.float32))
    carry, _ = lax.scan(page_step, init, (jnp.arange(n_pages), page_table.T))
    _, l, acc = update(carry, latent, pos)
    o_lat = (acc / l[..., None]).astype(xn.dtype)
    o = jnp.einsum('bthl,lhd->bthd', o_lat, w_uv)
    return o.reshape(Bd, T, ODD_MIX), latent


def setup_inputs(seed: int = 0) -> dict:
    key = jax.random.key(seed)
    ks = iter(jax.random.split(key, 40))

    def nrm(shape, scale=1.0):
        return jax.random.normal(next(ks), shape, jnp.float32) * scale

    def gain(shape):
        return 1.0 + 0.02 * jax.random.normal(next(ks), shape, jnp.float32)

    n_pages = PAST_LEN // PAGE_SIZE
    n_used = DEC_BATCH * n_pages
    n_pool = n_used + (n_used + 3) // 4
    wb = min(WINDOW, PAST_LEN)
    x_prompt = nrm((BATCH, SEQ, D_MODEL))
    x_sample = nrm((DEC_BATCH, DEC_SEQ, D_MODEL))
    cache_nsa_cmp_kv = nrm((N_EVEN, n_pool, PAGE_SIZE, 2, NSA_KV_HEADS, HEAD_DIM))
    cache_nsa_slc_kv = nrm((N_EVEN, n_pool, PAGE_SIZE, 2, NSA_KV_HEADS, HEAD_DIM))
    state_nsa_win_kv = nrm((N_EVEN, DEC_BATCH, wb, 2, NSA_KV_HEADS, HEAD_DIM))
    cache_dsa_kv = nrm((N_EVEN, n_pool, PAGE_SIZE, 2, DSA_KV_HEADS, HEAD_DIM))
    cache_dsa_idx_k = nrm((N_EVEN, n_pool, PAGE_SIZE, IDX_DIM))
    cache_mla_latent = nrm((N_ODD, n_pool, PAGE_SIZE, KV_LORA + MLA_ROPE))
    page_table = jax.random.permutation(next(ks), n_pool)[:n_used].reshape(DEC_BATCH, n_pages).astype(jnp.int32)
    return {
        'x_prompt': x_prompt,
        'x_sample': x_sample,
        'cache_nsa_cmp_kv': cache_nsa_cmp_kv,
        'cache_nsa_slc_kv': cache_nsa_slc_kv,
        'state_nsa_win_kv': state_nsa_win_kv,
        'cache_dsa_kv': cache_dsa_kv,
        'cache_dsa_idx_k': cache_dsa_idx_k,
        'cache_mla_latent': cache_mla_latent,
        'page_table': page_table,
        'norm_g': gain((DEPTH, 3, D_MODEL)),
        'ffn_w_gu': nrm((DEPTH, 2, D_MODEL, 2 * D_FF), D_MODEL ** -0.5),
        'ffn_w_down': nrm((DEPTH, 2, D_FF, D_MODEL), D_FF ** -0.5),
        'even_w_in': nrm((N_EVEN, D_MODEL, EVEN_IN), D_MODEL ** -0.5),
        'nsa_qk_g': gain((N_EVEN, 4, HEAD_DIM)),
        'nsa_cmp_pe': nrm((N_EVEN, 2, CMP_BLOCK, HEAD_DIM), 0.1),
        'nsa_cmp_w1': nrm((N_EVEN, 2, CMP_BLOCK * HEAD_DIM, CMP_HIDDEN), (CMP_BLOCK * HEAD_DIM) ** -0.5),
        'nsa_cmp_w2': nrm((N_EVEN, 2, CMP_HIDDEN, HEAD_DIM), CMP_HIDDEN ** -0.5),
        'dsa_qk_g': gain((N_EVEN, 2, HEAD_DIM)),
        'even_w_out': nrm((N_EVEN, EVEN_MIX, D_MODEL), EVEN_MIX ** -0.5),
        'odd_w_in': nrm((N_ODD, D_MODEL, ODD_IN), D_MODEL ** -0.5),
        'mla_norm_cq': gain((N_ODD, Q_LORA)),
        'mla_norm_ckv': gain((N_ODD, KV_LORA)),
        'mla_w_qb': nrm((N_ODD, Q_LORA, MLA_HEADS, MLA_QK), Q_LORA ** -0.5),
        'mla_w_uk': nrm((N_ODD, KV_LORA, MLA_HEADS, MLA_NOPE), KV_LORA ** -0.5),
        'mla_w_uv': nrm((N_ODD, KV_LORA, MLA_HEADS, MLA_V), KV_LORA ** -0.5),
        'mla_qk_g': gain((N_ODD, 2, MLA_QK)),
        'odd_w_out': nrm((N_ODD, ODD_MIX, D_MODEL), ODD_MIX ** -0.5),
    }


def reference(x_prompt, x_sample, cache_nsa_cmp_kv, cache_nsa_slc_kv, state_nsa_win_kv, cache_dsa_kv, cache_dsa_idx_k, cache_mla_latent, page_table, norm_g, ffn_w_gu, ffn_w_down, even_w_in, nsa_qk_g, nsa_cmp_pe, nsa_cmp_w1, nsa_cmp_w2, dsa_qk_g, even_w_out, odd_w_in, mla_norm_cq, mla_norm_ckv, mla_w_qb, mla_w_uk, mla_w_uv, mla_qk_g, odd_w_out):
    yp, ys = x_prompt, x_sample
    even_p = [[] for _ in range(5)]
    even_s = [[] for _ in range(5)]
    mla_p, mla_s = [], []
    for layer in range(DEPTH):
        li = layer // 2
        yp = macaron_half(yp, norm_g[layer, 0], ffn_w_gu[layer, 0], ffn_w_down[layer, 0])
        ys = macaron_half(ys, norm_g[layer, 0], ffn_w_gu[layer, 0], ffn_w_down[layer, 0])
        if layer % 2 == 0:
            w = (even_w_in[li], nsa_qk_g[li], dsa_qk_g[li], nsa_cmp_pe[li], nsa_cmp_w1[li], nsa_cmp_w2[li])
            o_p, rows_p = even_mix_prompt(rms_norm(yp, norm_g[layer, 1]), *w)
            o_s, rows_s = even_mix_sample(rms_norm(ys, norm_g[layer, 1]), *w, li, cache_nsa_cmp_kv, cache_nsa_slc_kv, state_nsa_win_kv, cache_dsa_kv, cache_dsa_idx_k, page_table)
            yp = yp + o_p @ even_w_out[li]
            ys = ys + o_s @ even_w_out[li]
            for lst, r in zip(even_p, rows_p):
                lst.append(r)
            for lst, r in zip(even_s, rows_s):
                lst.append(r)
        else:
            w = (odd_w_in[li], mla_norm_cq[li], mla_norm_ckv[li], mla_w_qb[li], mla_w_uk[li], mla_w_uv[li], mla_qk_g[li])
            o_p, lat_p = odd_mix_prompt(rms_norm(yp, norm_g[layer, 1]), *w)
            o_s, lat_s = odd_mix_sample(rms_norm(ys, norm_g[layer, 1]), *w, li, cache_mla_latent, page_table)
            yp = yp + o_p @ odd_w_out[li]
            ys = ys + o_s @ odd_w_out[li]
            mla_p.append(lat_p)
            mla_s.append(lat_s)
        yp = macaron_half(yp, norm_g[layer, 2], ffn_w_gu[layer, 1], ffn_w_down[layer, 1])
        ys = macaron_half(ys, norm_g[layer, 2], ffn_w_gu[layer, 1], ffn_w_down[layer, 1])
    cmp_p, slc_p, win_p, dsa_p, idx_p = [jnp.stack(a) for a in even_p]
    cmp_s, slc_s, win_s, dsa_s, idx_s = [jnp.stack(a) for a in even_s]
    mla_prompt = jnp.stack(mla_p)
    mla_sample = jnp.stack(mla_s)
    return (yp, ys, cmp_p, cmp_s, slc_p, slc_s, win_p, win_s, dsa_p, dsa_s, idx_p, idx_s, mla_prompt, mla_sample)
```

```python
import functools
import math

import numpy as np
import jax
import jax.numpy as jnp
from jax import lax
from jax.experimental import pallas as pl
from jax.experimental.pallas import tpu as pltpu

F32 = jnp.float32
BF16 = jnp.bfloat16

LANE = 128
SUBLANE = 8
MIB = 1024 * 1024

PAGE_SIZE = 128
HEAD_DIM = 64
ROT_DIM = HEAD_DIM // 4
ROPE_THETA = 500000.0
NORM_EPS = 1e-6
NEG_BIG = -1e30
NSA_KV_HEADS = 2
NSA_GROUP = 4
CMP_BLOCK = 32
SEL_BLOCK = 64
N_SEL = 16
WINDOW = 512
FORCE_BONUS = float(NSA_GROUP + 1)
DSA_TOPK = 256
IDX_HEADS = 4
IDX_DIM = 64
MLA_HEADS = 16
MLA_NOPE = 64
MLA_ROPE = 32
MLA_V = 64
MLA_QK = MLA_NOPE + MLA_ROPE
MLA_PAD = LANE
Q_LORA = 256
KV_LORA = 256
GATE_LANE0 = IDX_DIM
WI_LANE0 = GATE_LANE0 + 3 * NSA_KV_HEADS * NSA_GROUP
HI = lax.Precision.HIGHEST


def _params(sem, vmem_mib=48):
    return pltpu.CompilerParams(dimension_semantics=sem, vmem_limit_bytes=vmem_mib * MIB)


def _dot(a, b, precision=None):
    return jnp.dot(a, b, preferred_element_type=F32, precision=precision)


def _dot_nt(a, b):
    return lax.dot_general(a, b, (((1,), (1,)), ((), ())), preferred_element_type=F32)


def _rms_rows(x, g):
    ms = jnp.mean(x * x, axis=-1, keepdims=True)
    return x * lax.rsqrt(ms + NORM_EPS) * g


def _group_mean_sq(x, ones_bf16):
    sq = x * x
    hi = sq.astype(BF16)
    lo = (sq - hi.astype(F32)).astype(BF16)
    return _dot(hi, ones_bf16) + _dot(lo, ones_bf16)


def _rope_lanes(x, c, s1, s2, half):
    return x * c + pltpu.roll(x, LANE - half, 1) * s1 + pltpu.roll(x, half, 1) * s2


def _ffn_kernel(x_ref, g_ref, wa_ref, wu_ref, wd_ref, o_ref, xn_ref, acc_ref):
    f = pl.program_id(1)

    @pl.when(f == 0)
    def _():
        xn_ref[...] = _rms_rows(x_ref[...], g_ref[...]).astype(BF16)
        acc_ref[...] = jnp.zeros_like(acc_ref)

    xn = xn_ref[...]
    a = _dot(xn, wa_ref[...])
    u = _dot(xn, wu_ref[...])
    h = (a * jax.nn.sigmoid(a) * u).astype(BF16)
    acc_ref[...] += _dot(h, wd_ref[...])

    @pl.when(f == pl.num_programs(1) - 1)
    def _():
        o_ref[...] = x_ref[...] + 0.5 * acc_ref[...]


def _ffn_half(x, g, w_gu, w_down, tm, tf=256):
    m, d = x.shape
    d_ff = w_down.shape[0]
    n_f = d_ff // tf
    return pl.pallas_call(
        _ffn_kernel,
        grid=(m // tm, n_f),
        in_specs=[
            pl.BlockSpec((tm, d), lambda i, f: (i, 0)),
            pl.BlockSpec((1, d), lambda i, f: (0, 0)),
            pl.BlockSpec((d, tf), lambda i, f: (0, f)),
            pl.BlockSpec((d, tf), lambda i, f: (0, n_f + f)),
            pl.BlockSpec((tf, d), lambda i, f: (f, 0)),
        ],
        out_specs=pl.BlockSpec((tm, d), lambda i, f: (i, 0)),
        out_shape=jax.ShapeDtypeStruct((m, d), F32),
        scratch_shapes=[pltpu.VMEM((tm, d), BF16), pltpu.VMEM((tm, d), F32)],
        compiler_params=_params(("parallel", "arbitrary")),
        name="ffn_half",
    )(x, g.reshape(1, d), w_gu, w_gu, w_down)


EVEN_TILES = 19


def _even_proj_kernel(x_ref, g_ref, w_ref, gain_ref, c_ref, s1_ref, s2_ref, bd_ref,
                      qa_ref, cmp_ref, slc_ref, win_ref, qb_ref, kvb_ref, qi_ref, misc_ref, ki_ref):
    xn = _rms_rows(x_ref[...], g_ref[...]).astype(BF16)
    proj = _dot(xn, w_ref[...])
    c, s1, s2 = c_ref[...], s1_ref[...], s2_ref[...]
    bd = bd_ref[...]
    half = ROT_DIM // 2

    def tile(i):
        return proj[:, i * LANE:(i + 1) * LANE]

    def normed(i):
        x = tile(i)
        return x * lax.rsqrt(_group_mean_sq(x, bd) + NORM_EPS) * gain_ref[:, i * LANE:(i + 1) * LANE]

    def rope(x):
        return _rope_lanes(x, c, s1, s2, half)

    for i in range(4):
        qa_ref[:, i * LANE:(i + 1) * LANE] = rope(normed(i))
    for br, ref in enumerate((cmp_ref, slc_ref, win_ref)):
        ref[:, :LANE] = rope(normed(4 + 2 * br))
        ref[:, LANE:] = tile(5 + 2 * br)
    for i in range(4):
        qb_ref[:, i * LANE:(i + 1) * LANE] = rope(normed(10 + i))
    kvb_ref[:, :LANE] = rope(normed(14))
    kvb_ref[:, LANE:] = tile(15)
    for i in range(2):
        qi_ref[:, i * LANE:(i + 1) * LANE] = rope(tile(16 + i))
    m = tile(18)
    mr = rope(m)
    lane = lax.broadcasted_iota(jnp.int32, m.shape, 1)
    misc = jnp.where(lane < GATE_LANE0, mr,
                     jnp.where(lane < WI_LANE0, jax.nn.sigmoid(m),
                               jnp.where(lane < WI_LANE0 + IDX_HEADS, m * IDX_HEADS ** -0.5, 0.0)))
    misc_ref[...] = misc
    ki_ref[...] = mr[:, :IDX_DIM]


def _even_project(x, g, w, gain, tabs, tab_period, bd, tm):
    m, d = x.shape
    c, s1, s2 = tabs
    row = lambda i: (i, 0)
    fixed = lambda i: (0, 0)
    tab = lambda i: (i % tab_period, 0)
    widths = (512, 256, 256, 256, 512, 256, 256, LANE, IDX_DIM)
    return pl.pallas_call(
        _even_proj_kernel,
        grid=(m // tm,),
        in_specs=[
            pl.BlockSpec((tm, d), row),
            pl.BlockSpec((1, d), fixed),
            pl.BlockSpec(w.shape, fixed),
            pl.BlockSpec(gain.shape, fixed),
            pl.BlockSpec((tm, LANE), tab), pl.BlockSpec((tm, LANE), tab), pl.BlockSpec((tm, LANE), tab),
            pl.BlockSpec(bd.shape, fixed),
        ],
        out_specs=[pl.BlockSpec((tm, wd), row) for wd in widths],
        out_shape=[jax.ShapeDtypeStruct((m, wd), F32) for wd in widths],
        compiler_params=_params(("parallel",)),
        name="even_project",
    )(x, g.reshape(1, d), w, gain, c, s1, s2, bd)


def _compress_rows(row_refs, n_blk, pe_ref, w1_ref, w2_ref):
    outs = []
    for k in range(2):
        acc = jnp.zeros((n_blk, 2 * LANE), F32)
        for j in range(CMP_BLOCK):
            xj = row_refs[k][pl.ds(j, n_blk, stride=CMP_BLOCK), :] + pe_ref[j:j + 1, k * LANE:(k + 1) * LANE]
            acc = acc + _dot(xj.astype(BF16), w1_ref[k, j])
        outs.append(_dot(jax.nn.gelu(acc).astype(BF16), w2_ref[k]))
    return jnp.concatenate(outs, axis=-1)


def _compress_kernel(k_ref, v_ref, pe_ref, w1_ref, w2_ref, o_ref):
    o_ref[...] = _compress_rows((k_ref, v_ref), o_ref.shape[0], pe_ref, w1_ref, w2_ref)


def _compress_dense(rows, pe, w1, w2, t_rows):
    m = rows.shape[0]
    n_blk = t_rows // CMP_BLOCK
    return pl.pallas_call(
        _compress_kernel,
        grid=(m // t_rows,),
        in_specs=[
            pl.BlockSpec((t_rows, LANE), lambda i: (i, 0)),
            pl.BlockSpec((t_rows, LANE), lambda i: (i, 1)),
            pl.BlockSpec(pe.shape, lambda i: (0, 0)),
            pl.BlockSpec(w1.shape, lambda i: (0, 0, 0, 0)),
            pl.BlockSpec(w2.shape, lambda i: (0, 0, 0)),
        ],
        out_specs=pl.BlockSpec((n_blk, 2 * LANE), lambda i: (i, 0)),
        out_shape=jax.ShapeDtypeStruct((m // CMP_BLOCK, 2 * LANE), F32),
        compiler_params=_params(("parallel",)),
        name="nsa_compress_dense",
    )(rows, rows, pe, w1, w2)


def _page_specs(n_per_step, n_pages, width):
    return [pl.BlockSpec((None, PAGE_SIZE, width),
                         lambda b, c, pt, p=p: (pt[b * n_pages + c * n_per_step + p], 0, 0))
            for p in range(n_per_step)]


def _compress_paged_kernel(pt_ref, pe_ref, w1_ref, w2_ref, *rest, n_per_step):
    pages = rest[:n_per_step]
    o_ref, kbuf_ref, vbuf_ref = rest[n_per_step:]
    for p in range(n_per_step):
        kbuf_ref[p * PAGE_SIZE:(p + 1) * PAGE_SIZE, :] = pages[p][:, :LANE]
        vbuf_ref[p * PAGE_SIZE:(p + 1) * PAGE_SIZE, :] = pages[p][:, LANE:]
    o_ref[...] = _compress_rows((kbuf_ref, vbuf_ref), o_ref.shape[0], pe_ref, w1_ref, w2_ref)


def _compress_paged(cache, pt_flat, n_seq, n_pages, pe, w1, w2, n_per_step):
    blk_per_step = n_per_step * PAGE_SIZE // CMP_BLOCK
    n_ch = n_pages // n_per_step
    grid_spec = pltpu.PrefetchScalarGridSpec(
        num_scalar_prefetch=1,
        grid=(n_seq, n_ch),
        in_specs=[
            pl.BlockSpec(pe.shape, lambda b, c, pt: (0, 0)),
            pl.BlockSpec(w1.shape, lambda b, c, pt: (0, 0, 0, 0)),
            pl.BlockSpec(w2.shape, lambda b, c, pt: (0, 0, 0)),
        ] + _page_specs(n_per_step, n_pages, 2 * LANE),
        out_specs=pl.BlockSpec((blk_per_step, 2 * LANE), lambda b, c, pt: (b * n_ch + c, 0)),
        scratch_shapes=[pltpu.VMEM((n_per_step * PAGE_SIZE, LANE), F32)] * 2,
    )
    return pl.pallas_call(
        functools.partial(_compress_paged_kernel, n_per_step=n_per_step),
        grid_spec=grid_spec,
        out_shape=jax.ShapeDtypeStruct((n_seq * n_pages * PAGE_SIZE // CMP_BLOCK, 2 * LANE), F32),
        compiler_params=_params(("parallel", "arbitrary")),
        name="nsa_compress_paged",
    )(pt_flat, pe, w1, w2, *([cache] * n_per_step))


def _topk_mask(score, idx, k, idx_bits):
    def count(pred):
        v = jnp.where(pred, 1.0, 0.0)
        if v.ndim == 3:
            v = jnp.sum(v, axis=0)
        return jnp.sum(v, axis=-1, keepdims=True)

    rows = score.shape[-2]
    bits = pltpu.bitcast(score + 0.0, jnp.int32)
    key = jnp.where(bits < 0, bits ^ jnp.int32(0x7FFFFFFF), bits)
    kf = float(k)
    int_min = jnp.full((rows, 1), -2 ** 31, jnp.int32)
    t0 = jnp.where(count(key >= 0) >= kf, jnp.zeros((rows, 1), jnp.int32), int_min)

    def value_bit(i, t):
        cand = t | jnp.left_shift(jnp.int32(1), 30 - i)
        return jnp.where(count(key >= cand) >= kf, cand, t)

    thr = lax.fori_loop(0, 31, value_bit, t0)
    need = kf - count(key > thr)
    tie_idx = jnp.where(key == thr, idx, jnp.int32(2 ** 30))

    def index_bit(i, j):
        cand = j | jnp.left_shift(jnp.int32(1), idx_bits - 1 - i)
        return jnp.where(count(tie_idx < cand) <= need, cand, j)

    j_cut = lax.fori_loop(0, idx_bits, index_bit, jnp.zeros((rows, 1), jnp.int32))
    return jnp.where(key > thr, 1.0, jnp.where(tie_idx < j_cut, 1.0, 0.0))


def _nsa_cmp_kernel(q_ref, kvc_ref, pm_ref, oc_ref, mask_ref, *, tq, pos0, n_keys):
    nc = kvc_ref.shape[1]
    nsp = pm_ref.shape[1]
    lk = mask_ref.shape[3]
    ns = -(-n_keys // SEL_BLOCK)
    n_sel = min(N_SEL, ns)
    r_rows = NSA_GROUP * tq
    q = q_ref[0]
    kvc = kvc_ref[0]
    pos_t = pos0 + pl.program_id(1) * tq + lax.broadcasted_iota(jnp.int32, (tq, 1), 0)
    pos_r = jnp.concatenate([pos_t] * NSA_GROUP, axis=0)
    c_end = (lax.broadcasted_iota(jnp.int32, (r_rows, nc), 1) + 1) * CMP_BLOCK - 1
    vis = c_end <= pos_r
    blk = lax.broadcasted_iota(jnp.int32, (tq, nsp), 1)
    cur = pos_t // SEL_BLOCK
    forced = jnp.where(blk == 0, FORCE_BONUS, jnp.where(blk == cur, FORCE_BONUS, jnp.where(blk == cur - 1, FORCE_BONUS, 0.0)))
    lane = lax.broadcasted_iota(jnp.int32, (tq, LANE), 1)
    for g in range(NSA_KV_HEADS):
        qg = jnp.concatenate([q[:, (g * NSA_GROUP + r) * HEAD_DIM:(g * NSA_GROUP + r + 1) * HEAD_DIM]
                              for r in range(NSA_GROUP)], axis=0).astype(BF16)
        kg = kvc[:, g * HEAD_DIM:(g + 1) * HEAD_DIM].astype(BF16)
        vg = kvc[:, LANE + g * HEAD_DIM:LANE + (g + 1) * HEAD_DIM].astype(BF16)
        s = jnp.where(vis, _dot_nt(qg, kg) * HEAD_DIM ** -0.5, -jnp.inf)
        mx = jnp.max(s, axis=-1, keepdims=True)
        mx = jnp.where(mx > -jnp.inf, mx, 0.0)
        e = jnp.where(vis, jnp.exp(s - mx), 0.0)
        p = e / jnp.maximum(jnp.sum(e, axis=-1, keepdims=True), 1e-30)
        o = _dot(p.astype(BF16), vg)
        for r in range(NSA_GROUP):
            h = g * NSA_GROUP + r
            oc_ref[0, :, h * HEAD_DIM:(h + 1) * HEAD_DIM] = o[r * tq:(r + 1) * tq]
        imp = p[0:tq]
        for r in range(1, NSA_GROUP):
            imp = imp + p[r * tq:(r + 1) * tq]
        imp = _dot(imp, pm_ref[...], precision=HI)
        score = jnp.where(blk * SEL_BLOCK <= pos_t, imp + forced, -1.0)
        score = jnp.where(blk < ns, score, -jnp.inf)
        sel = _topk_mask(score, blk, n_sel, max(1, (nsp - 1).bit_length()))
        sel = jnp.where(score >= 0.0, sel, 0.0)
        for c in range(lk // LANE):
            lo = jnp.broadcast_to(sel[:, 2 * c:2 * c + 1], (tq, LANE))
            hi = jnp.broadcast_to(sel[:, 2 * c + 1:2 * c + 2], (tq, LANE))
            chunk = jnp.where(lane < SEL_BLOCK, lo, hi)
            chunk = jnp.where(c * LANE + lane <= pos_t, chunk, 0.0)
            mask_ref[0, g, :, c * LANE:(c + 1) * LANE] = chunk.astype(BF16)


def _nsa_cmp_select(q, kvc, pm, tq, pos0, n_keys, lk):
    b, sq, _ = q.shape
    nc = kvc.shape[1]
    return pl.pallas_call(
        functools.partial(_nsa_cmp_kernel, tq=tq, pos0=pos0, n_keys=n_keys),
        grid=(b, sq // tq),
        in_specs=[
            pl.BlockSpec((1, tq, 512), lambda i, j: (i, j, 0)),
            pl.BlockSpec((1, nc, 2 * LANE), lambda i, j: (i, 0, 0)),
            pl.BlockSpec(pm.shape, lambda i, j: (0, 0)),
        ],
        out_specs=[
            pl.BlockSpec((1, tq, 512), lambda i, j: (i, j, 0)),
            pl.BlockSpec((1, NSA_KV_HEADS, tq, lk), lambda i, j: (i, 0, j, 0)),
        ],
        out_shape=[jax.ShapeDtypeStruct((b, sq, 512), F32),
                   jax.ShapeDtypeStruct((b, NSA_KV_HEADS, sq, lk), BF16)],
        compiler_params=_params(("parallel", "parallel")),
        name="nsa_cmp_select",
    )(q, kvc, pm)


def _idx_scores(qi, wi, ki_b):
    out = None
    for h in range(IDX_HEADS):
        lg = _dot_nt(qi[:, h * IDX_DIM:(h + 1) * IDX_DIM].astype(BF16), ki_b) * IDX_DIM ** -0.5
        term = wi[:, WI_LANE0 + h:WI_LANE0 + h + 1] * jnp.maximum(lg, 0.0)
        out = term if out is None else out + term
    return out


def _dsa_index_kernel(qi_ref, mq_ref, mk_ref, mask_ref, *, tq, k_top):
    sk = mk_ref.shape[1]
    ki = mk_ref[0][:, :IDX_DIM].astype(BF16)
    score = _idx_scores(qi_ref[0], mq_ref[0], ki)
    pos_q = pl.program_id(1) * tq + lax.broadcasted_iota(jnp.int32, (tq, 1), 0)
    pos_k = lax.broadcasted_iota(jnp.int32, (tq, sk), 1)
    score = jnp.where(pos_k <= pos_q, score, -jnp.inf)
    sel = _topk_mask(score, pos_k, k_top, sk.bit_length())
    mask_ref[0, 0] = jnp.where(score > -jnp.inf, sel, 0.0).astype(BF16)


def _dsa_index_dense(qi, misc, tq, k_top):
    b, s, _ = qi.shape
    return pl.pallas_call(
        functools.partial(_dsa_index_kernel, tq=tq, k_top=k_top),
        grid=(b, s // tq),
        in_specs=[
            pl.BlockSpec((1, tq, IDX_HEADS * IDX_DIM), lambda i, j: (i, j, 0)),
            pl.BlockSpec((1, tq, LANE), lambda i, j: (i, j, 0)),
            pl.BlockSpec((1, s, LANE), lambda i, j: (i, 0, 0)),
        ],
        out_specs=pl.BlockSpec((1, 1, tq, s), lambda i, j: (i, 0, j, 0)),
        out_shape=jax.ShapeDtypeStruct((b, 1, s, s), BF16),
        compiler_params=_params(("parallel", "parallel")),
        name="dsa_index_dense",
    )(qi, misc, misc)


def _dsa_index_paged_kernel(pt_ref, qi_ref, mq_ref, new_ref, *rest, n_per_step, n_pages, t_new, k_top):
    pages = rest[:n_per_step]
    mask_ref, sc_ref = rest[n_per_step], rest[n_per_step + 1]
    c = pl.program_id(1)
    n_tiles = n_pages + 1
    qi, mq = qi_ref[0], mq_ref[0]
    for p in range(n_per_step):
        sc_ref[c * n_per_step + p] = _idx_scores(qi, mq, pages[p][...].astype(BF16))

    @pl.when(c == pl.num_programs(1) - 1)
    def _():
        s_new = _idx_scores(qi, mq, new_ref[0][:, :IDX_DIM].astype(BF16))
        row = lax.broadcasted_iota(jnp.int32, (t_new, LANE), 0)
        col = lax.broadcasted_iota(jnp.int32, (t_new, LANE), 1)
        sc_ref[n_pages] = jnp.where(col <= row, s_new, -jnp.inf)
        score = sc_ref[0:n_tiles]
        idx = (lax.broadcasted_iota(jnp.int32, score.shape, 0) * LANE
               + lax.broadcasted_iota(jnp.int32, score.shape, 2))
        sel = _topk_mask(score, idx, k_top, (n_tiles * LANE).bit_length())
        sel = jnp.where(score > -jnp.inf, sel, 0.0).astype(BF16)
        for i in range(n_tiles):
            mask_ref[0, 0, :, i * LANE:(i + 1) * LANE] = sel[i]


def _dsa_index_paged(qi, misc, misc_new_pad, cache_idx, pt_flat, n_pages, n_per_step, k_top):
    b, t, _ = qi.shape
    lk = (n_pages + 1) * LANE
    grid_spec = pltpu.PrefetchScalarGridSpec(
        num_scalar_prefetch=1,
        grid=(b, n_pages // n_per_step),
        in_specs=[
            pl.BlockSpec((1, t, IDX_HEADS * IDX_DIM), lambda i, c, pt: (i, 0, 0)),
            pl.BlockSpec((1, t, LANE), lambda i, c, pt: (i, 0, 0)),
            pl.BlockSpec((1, PAGE_SIZE, LANE), lambda i, c, pt: (i, 0, 0)),
        ] + _page_specs(n_per_step, n_pages, IDX_DIM),
        out_specs=pl.BlockSpec((1, 1, t, lk), lambda i, c, pt: (i, 0, 0, 0)),
        scratch_shapes=[pltpu.VMEM((n_pages + 1, t, LANE), F32)],
    )
    return pl.pallas_call(
        functools.partial(_dsa_index_paged_kernel, n_per_step=n_per_step, n_pages=n_pages, t_new=t, k_top=k_top),
        grid_spec=grid_spec,
        out_shape=jax.ShapeDtypeStruct((b, 1, t, lk), BF16),
        compiler_params=_params(("parallel", "arbitrary")),
        name="dsa_index_paged",
    )(pt_flat, qi, misc, misc_new_pad, *([cache_idx] * n_per_step))


def _stack_heads(q, g, n_rep, dk):
    return jnp.concatenate([q[:, (g * n_rep + r) * dk:(g * n_rep + r + 1) * dk] for r in range(n_rep)],
                           axis=0).astype(BF16)


def _online_update(s, ok, g, m_ref, l_ref, acc_ref, pv):
    s = jnp.where(ok, s, NEG_BIG)
    m_prev = m_ref[g]
    m_new = jnp.maximum(m_prev, jnp.max(s, axis=-1, keepdims=True))
    p = jnp.where(ok, jnp.exp(s - m_new), 0.0)
    corr = jnp.exp(m_prev - m_new)
    l_ref[g] = l_ref[g] * corr + jnp.sum(p, axis=-1, keepdims=True)
    acc_ref[g] = acc_ref[g] * corr + pv(p.astype(BF16))
    m_ref[g] = m_new


def _attn_init(m_ref, l_ref, acc_ref):
    m_ref[...] = jnp.full(m_ref.shape, NEG_BIG, F32)
    l_ref[...] = jnp.zeros_like(l_ref)
    acc_ref[...] = jnp.zeros_like(acc_ref)


def _attn_finish(o_ref, l_ref, acc_ref, n_grp, n_rep, tq, dv):
    for g in range(n_grp):
        o = acc_ref[g] / jnp.maximum(l_ref[g], 1e-30)
        for r in range(n_rep):
            h = g * n_rep + r
            o_ref[0, :, h * dv:(h + 1) * dv] = o[r * tq:(r + 1) * tq]


def _tile_rows(x, n_rep):
    return x if n_rep == 1 else jnp.concatenate([x] * n_rep, axis=0)


def _attn_kernel(*refs, n_grp, n_rep, dk, dv, scale, mode, tq, tk, window, mask_groups, skip):
    if mode == "array":
        q_ref, k_ref, v_ref, mask_ref, o_ref, m_ref, l_ref, acc_ref = refs
    else:
        q_ref, k_ref, v_ref, o_ref, m_ref, l_ref, acc_ref = refs
        mask_ref = None
    qi, kc = pl.program_id(1), pl.program_id(2)

    @pl.when(kc == 0)
    def _():
        _attn_init(m_ref, l_ref, acc_ref)

    q_lo = qi * tq
    relevant = kc * tk <= q_lo + tq - 1 if skip else kc >= 0
    if mode == "window":
        relevant = jnp.logical_and(relevant, kc * tk + tk - 1 >= q_lo - window)

    @pl.when(relevant)
    def _():
        q, k, v = q_ref[0], k_ref[0], v_ref[0]
        if mode != "array":
            pos_q = _tile_rows(q_lo + lax.broadcasted_iota(jnp.int32, (tq, 1), 0), n_rep)
            dpos = pos_q - (kc * tk + lax.broadcasted_iota(jnp.int32, (n_rep * tq, tk), 1))
            ok_pos = dpos >= 0 if mode == "causal" else jnp.abs(2 * dpos - window) <= window
        for g in range(n_grp):
            qg = _stack_heads(q, g, n_rep, dk)
            kg = k[:, g * dk:(g + 1) * dk].astype(BF16)
            vg = v[:, g * dv:(g + 1) * dv].astype(BF16)
            s = _dot_nt(qg, kg) * scale
            if mode == "array":
                ok = _tile_rows(mask_ref[0, g if mask_groups > 1 else 0].astype(F32), n_rep) > 0.5
            else:
                ok = ok_pos
            _online_update(s, ok, g, m_ref, l_ref, acc_ref, lambda pb, vg=vg: _dot(pb, vg))

    @pl.when(kc == pl.num_programs(2) - 1)
    def _():
        _attn_finish(o_ref, l_ref, acc_ref, n_grp, n_rep, tq, dv)


def _attention(q, k, v, mask, *, n_grp, n_rep, dk, dv, scale, mode, tq, tk, k_tile=0, v_tile=0, skip=True):
    b, sq, _ = q.shape
    sk = k.shape[1]
    n_kc = sk // tk

    def kc_eff(j, c):
        if not skip:
            return c
        hi = (j * tq + tq - 1) // tk
        c = jnp.minimum(c, hi)
        if mode == "window":
            c = jnp.maximum(c, jnp.maximum(j * tq - WINDOW, 0) // tk)
        return c

    in_specs = [
        pl.BlockSpec((1, tq, n_grp * n_rep * dk), lambda i, j, c: (i, j, 0)),
        pl.BlockSpec((1, tk, n_grp * dk), lambda i, j, c: (i, kc_eff(j, c), k_tile)),
        pl.BlockSpec((1, tk, n_grp * dv), lambda i, j, c: (i, kc_eff(j, c), v_tile)),
    ]
    args = [q, k, v]
    mask_groups = 1
    if mode == "array":
        mb, mask_groups = mask.shape[0], mask.shape[1]
        in_specs.append(pl.BlockSpec((1, mask_groups, tq, tk),
                                     lambda i, j, c: (i if mb > 1 else 0, 0, j, kc_eff(j, c))))
        args.append(mask)
    rows = n_rep * tq
    kern = functools.partial(_attn_kernel, n_grp=n_grp, n_rep=n_rep, dk=dk, dv=dv, scale=scale, mode=mode,
                             tq=tq, tk=tk, window=WINDOW, mask_groups=mask_groups, skip=skip)
    return pl.pallas_call(
        kern,
        grid=(b, sq // tq, n_kc),
        in_specs=in_specs,
        out_specs=pl.BlockSpec((1, tq, n_grp * n_rep * dv), lambda i, j, c: (i, j, 0)),
        out_shape=jax.ShapeDtypeStruct((b, sq, n_grp * n_rep * dv), F32),
        scratch_shapes=[pltpu.VMEM((n_grp, rows, 1), F32), pltpu.VMEM((n_grp, rows, 1), F32),
                        pltpu.VMEM((n_grp, rows, dv), F32)],
        compiler_params=_params(("parallel", "parallel", "arbitrary")),
        name="attention_" + mode,
    )(*args)


def _paged_attn_kernel(pt_ref, q_ref, mask_ref, mask_new_ref, new_ref, *rest, n_per_step, n_grp, n_rep, mask_groups):
    pages = rest[:n_per_step]
    o_ref, m_ref, l_ref, acc_ref = rest[n_per_step:]
    c = pl.program_id(1)
    t = q_ref.shape[1]
    scale = HEAD_DIM ** -0.5

    @pl.when(c == 0)
    def _():
        _attn_init(m_ref, l_ref, acc_ref)

    q = q_ref[0]

    def k_of(page, g):
        return page[:, g * HEAD_DIM:(g + 1) * HEAD_DIM].astype(BF16)

    def v_of(page, g):
        return page[:, LANE + g * HEAD_DIM:LANE + (g + 1) * HEAD_DIM].astype(BF16)

    for g in range(n_grp):
        qg = _stack_heads(q, g, n_rep, HEAD_DIM)
        s = jnp.concatenate([_dot_nt(qg, k_of(pages[p][...], g)) for p in range(n_per_step)], axis=1) * scale
        ok = _tile_rows(mask_ref[0, g if mask_groups > 1 else 0].astype(F32), n_rep) > 0.5

        def pv(pb, g=g):
            out = None
            for p in range(n_per_step):
                term = _dot(pb[:, p * PAGE_SIZE:(p + 1) * PAGE_SIZE], v_of(pages[p][...], g))
                out = term if out is None else out + term
            return out

        _online_update(s, ok, g, m_ref, l_ref, acc_ref, pv)

    @pl.when(c == pl.num_programs(1) - 1)
    def _():
        new = new_ref[0]
        for g in range(n_grp):
            qg = _stack_heads(q, g, n_rep, HEAD_DIM)
            s = _dot_nt(qg, k_of(new, g)) * scale
            ok = _tile_rows(mask_new_ref[0, g if mask_groups > 1 else 0].astype(F32), n_rep) > 0.5
            _online_update(s, ok, g, m_ref, l_ref, acc_ref, lambda pb, g=g: _dot(pb, v_of(new, g)))
        _attn_finish(o_ref, l_ref, acc_ref, n_grp, n_rep, t, HEAD_DIM)


def _paged_attention(q, mask, new_pad, cache, pt_flat, n_pages, n_per_step, n_grp, n_rep):
    b, t, hd = q.shape
    mask_groups = mask.shape[1]
    rows = n_rep * t
    grid_spec = pltpu.PrefetchScalarGridSpec(
        num_scalar_prefetch=1,
        grid=(b, n_pages // n_per_step),
        in_specs=[
            pl.BlockSpec((1, t, hd), lambda i, c, pt: (i, 0, 0)),
            pl.BlockSpec((1, mask_groups, t, n_per_step * PAGE_SIZE), lambda i, c, pt: (i, 0, 0, c)),
            pl.BlockSpec((1, mask_groups, t, PAGE_SIZE), lambda i, c, pt: (i, 0, 0, n_pages)),
            pl.BlockSpec((1, PAGE_SIZE, 2 * LANE), lambda i, c, pt: (i, 0, 0)),
        ] + _page_specs(n_per_step, n_pages, 2 * LANE),
        out_specs=pl.BlockSpec((1, t, hd), lambda i, c, pt: (i, 0, 0)),
        scratch_shapes=[pltpu.VMEM((n_grp, rows, 1), F32), pltpu.VMEM((n_grp, rows, 1), F32),
                        pltpu.VMEM((n_grp, rows, HEAD_DIM), F32)],
    )
    return pl.pallas_call(
        functools.partial(_paged_attn_kernel, n_per_step=n_per_step, n_grp=n_grp, n_rep=n_rep,
                          mask_groups=mask_groups),
        grid_spec=grid_spec,
        out_shape=jax.ShapeDtypeStruct((b, t, hd), F32),
        compiler_params=_params(("parallel", "arbitrary")),
        name="paged_attention",
    )(pt_flat, q, mask, mask, new_pad, *([cache] * n_per_step))


def _odd_proj_kernel(*refs, with_kv):
    if with_kv:
        (x_ref, g_ref, w_ref, ncq_ref, nckv_ref, wqb_ref, gq_ref, c_ref, s1_ref, s2_ref, ones_ref,
         wuk_ref, gk_ref, wuv_ref, q_ref, lat_ref, k_ref, v_ref) = refs
    else:
        (x_ref, g_ref, w_ref, ncq_ref, nckv_ref, wqb_ref, gq_ref, c_ref, s1_ref, s2_ref, ones_ref,
         q_ref, lat_ref) = refs
    xn = _rms_rows(x_ref[...], g_ref[...]).astype(BF16)
    proj = _dot(xn, w_ref[...])
    c, s1, s2 = c_ref[...], s1_ref[...], s2_ref[...]
    ones = ones_ref[...]
    half = MLA_ROPE // 2

    def head_norm(x, gain):
        return x * lax.rsqrt(_group_mean_sq(x, ones) + NORM_EPS) * gain

    cqn = _rms_rows(proj[:, :Q_LORA], ncq_ref[...]).astype(BF16)
    qp = _dot(cqn, wqb_ref[...])
    c_lat = _rms_rows(proj[:, Q_LORA:Q_LORA + KV_LORA], nckv_ref[...])
    kr = _rope_lanes(proj[:, Q_LORA + KV_LORA:], c, s1, s2, half)
    lat_ref[:, :KV_LORA] = c_lat
    lat_ref[:, KV_LORA:] = kr[:, MLA_NOPE:MLA_QK]
    for h in range(MLA_HEADS):
        sl = slice(h * MLA_PAD, (h + 1) * MLA_PAD)
        q_ref[:, sl] = head_norm(_rope_lanes(qp[:, sl], c, s1, s2, half), gq_ref[:, sl])
    if with_kv:
        cb = c_lat.astype(BF16)
        kn = _dot(cb, wuk_ref[...])
        for h in range(MLA_HEADS):
            sl = slice(h * MLA_PAD, (h + 1) * MLA_PAD)
            k_ref[:, sl] = head_norm(kn[:, sl] + kr, gk_ref[:, sl])
        v_ref[...] = _dot(cb, wuv_ref[...])


def _odd_project(x, g, w, ncq, nckv, wqb, gq, tabs, tab_period, ones, kv_weights, tm):
    m, d = x.shape
    c, s1, s2 = tabs
    row = lambda i: (i, 0)
    fixed = lambda i: (0, 0)
    tab = lambda i: (i % tab_period, 0)
    with_kv = kv_weights is not None
    args = [x, g.reshape(1, d), w, ncq, nckv, wqb, gq, c, s1, s2, ones]
    in_specs = [pl.BlockSpec((tm, d), row), pl.BlockSpec((1, d), fixed), pl.BlockSpec(w.shape, fixed),
                pl.BlockSpec(ncq.shape, fixed), pl.BlockSpec(nckv.shape, fixed), pl.BlockSpec(wqb.shape, fixed),
                pl.BlockSpec(gq.shape, fixed),
                pl.BlockSpec((tm, LANE), tab), pl.BlockSpec((tm, LANE), tab), pl.BlockSpec((tm, LANE), tab),
                pl.BlockSpec(ones.shape, fixed)]
    widths = [MLA_HEADS * MLA_PAD, KV_LORA + MLA_ROPE]
    if with_kv:
        args += list(kv_weights)
        in_specs += [pl.BlockSpec(a.shape, fixed) for a in kv_weights]
        widths += [MLA_HEADS * MLA_PAD, MLA_HEADS * MLA_V]
    return pl.pallas_call(
        functools.partial(_odd_proj_kernel, with_kv=with_kv),
        grid=(m // tm,),
        in_specs=in_specs,
        out_specs=[pl.BlockSpec((tm, wd), row) for wd in widths],
        out_shape=[jax.ShapeDtypeStruct((m, wd), F32) for wd in widths],
        compiler_params=_params(("parallel",)),
        name="odd_project",
    )(*args)


def _mla_paged_kernel(pt_ref, q_ref, gk_ref, wuk_ref, wuv_ref, new_ref, *rest, n_per_step):
    pages = rest[:n_per_step]
    o_ref, buf_ref, qbd_ref, qr_ref, bdt_ref, m_ref, l_ref, acc_ref = rest[n_per_step:]
    c = pl.program_id(1)
    t = q_ref.shape[1]
    rows = MLA_HEADS * t
    d_nope = MLA_HEADS * MLA_NOPE
    scale = MLA_QK ** -0.5

    def head_diag(shape):
        return (lax.broadcasted_iota(jnp.int32, shape, 0) // t
                == lax.broadcasted_iota(jnp.int32, shape, 1) // MLA_NOPE)

    @pl.when(c == 0)
    def _():
        qg = q_ref[0] * gk_ref[...]
        nope = jnp.concatenate([qg[:, h * MLA_PAD:h * MLA_PAD + MLA_NOPE] for h in range(MLA_HEADS)], axis=1)
        diag = head_diag((rows, d_nope))
        qbd_ref[...] = jnp.where(diag, _tile_rows(nope, MLA_HEADS), 0.0).astype(BF16)
        bdt_ref[...] = jnp.where(diag, 1.0, 0.0).astype(BF16)
        qr_ref[...] = jnp.concatenate([qg[:, h * MLA_PAD + MLA_NOPE:h * MLA_PAD + MLA_QK]
                                       for h in range(MLA_HEADS)], axis=0).astype(BF16)
        m_ref[...] = jnp.full(m_ref.shape, NEG_BIG, F32)
        l_ref[...] = jnp.zeros_like(l_ref)
        acc_ref[...] = jnp.zeros_like(acc_ref)

    def process(lat, ok):
        cb = lat[:, :KV_LORA].astype(BF16)
        kr = lat[:, KV_LORA:]
        kn = _dot(cb, wuk_ref[...])
        s = _dot_nt(qbd_ref[...], kn.astype(BF16)) + _dot_nt(qr_ref[...], kr.astype(BF16))
        ss = (_dot_nt(bdt_ref[...], (kn * kn).astype(BF16))
              + _dot_nt(jnp.ones((rows, MLA_ROPE), BF16), (kr * kr).astype(BF16)))
        s = s * lax.rsqrt(ss * (1.0 / MLA_QK) + NORM_EPS) * scale
        if ok is not None:
            s = jnp.where(ok, s, NEG_BIG)
        m_prev = m_ref[...]
        m_new = jnp.maximum(m_prev, jnp.max(s, axis=-1, keepdims=True))
        p = jnp.exp(s - m_new)
        if ok is not None:
            p = jnp.where(ok, p, 0.0)
        corr = jnp.exp(m_prev - m_new)
        l_ref[...] = l_ref[...] * corr + jnp.sum(p, axis=-1, keepdims=True)
        acc_ref[...] = acc_ref[...] * corr + _dot(p.astype(BF16), cb)
        m_ref[...] = m_new

    for p in range(n_per_step):
        buf_ref[p * PAGE_SIZE:(p + 1) * PAGE_SIZE, :] = pages[p][...]
    process(buf_ref[...], None)

    @pl.when(c == pl.num_programs(1) - 1)
    def _():
        q_t = lax.broadcasted_iota(jnp.int32, (rows, PAGE_SIZE), 0) % t
        k_t = lax.broadcasted_iota(jnp.int32, (rows, PAGE_SIZE), 1)
        process(new_ref[0], k_t <= q_t)
        o_lat = (acc_ref[...] / l_ref[...]).astype(BF16)
        full = jnp.where(head_diag((rows, MLA_HEADS * MLA_V)), _dot(o_lat, wuv_ref[...]), 0.0)
        out = full[0:t]
        for h in range(1, MLA_HEADS):
            out = out + full[h * t:(h + 1) * t]
        o_ref[0] = out


def _mla_paged(q, gk, wuk, wuv, new_pad, cache, pt_flat, n_pages, n_per_step):
    b, t, qd = q.shape
    rows = MLA_HEADS * t
    lat_w = KV_LORA + MLA_ROPE
    grid_spec = pltpu.PrefetchScalarGridSpec(
        num_scalar_prefetch=1,
        grid=(b, n_pages // n_per_step),
        in_specs=[
            pl.BlockSpec((1, t, qd), lambda i, c, pt: (i, 0, 0)),
            pl.BlockSpec(gk.shape, lambda i, c, pt: (0, 0)),
            pl.BlockSpec(wuk.shape, lambda i, c, pt: (0, 0)),
            pl.BlockSpec(wuv.shape, lambda i, c, pt: (0, 0)),
            pl.BlockSpec((1, PAGE_SIZE, lat_w), lambda i, c, pt: (i, 0, 0)),
        ] + _page_specs(n_per_step, n_pages, lat_w),
        out_specs=pl.BlockSpec((1, t, MLA_HEADS * MLA_V), lambda i, c, pt: (i, 0, 0)),
        scratch_shapes=[
            pltpu.VMEM((n_per_step * PAGE_SIZE, lat_w), F32),
            pltpu.VMEM((rows, MLA_HEADS * MLA_NOPE), BF16),
            pltpu.VMEM((rows, MLA_ROPE), BF16),
            pltpu.VMEM((rows, MLA_HEADS * MLA_NOPE), BF16),
            pltpu.VMEM((rows, 1), F32), pltpu.VMEM((rows, 1), F32), pltpu.VMEM((rows, KV_LORA), F32),
        ],
    )
    return pl.pallas_call(
        functools.partial(_mla_paged_kernel, n_per_step=n_per_step),
        grid_spec=grid_spec,
        out_shape=jax.ShapeDtypeStruct((b, t, MLA_HEADS * MLA_V), F32),
        compiler_params=_params(("parallel", "arbitrary")),
        name="mla_paged",
    )(pt_flat, q, gk, wuk, wuv, new_pad, *([cache] * n_per_step))


def _even_out_kernel(y_ref, oc_ref, os_ref, ow_ref, ob_ref, misc_ref, w_ref, o_ref):
    gate = misc_ref[...]
    tm = gate.shape[0]
    lane = lax.broadcasted_iota(jnp.int32, (tm, LANE), 1)

    def gate_pair(i, j):
        a = GATE_LANE0 + 3 * (2 * i) + j
        b = GATE_LANE0 + 3 * (2 * i + 1) + j
        return jnp.where(lane < HEAD_DIM, jnp.broadcast_to(gate[:, a:a + 1], (tm, LANE)),
                         jnp.broadcast_to(gate[:, b:b + 1], (tm, LANE)))

    n_a = oc_ref.shape[1] // LANE
    acc = y_ref[...]
    for i in range(n_a):
        sl = slice(i * LANE, (i + 1) * LANE)
        oa = gate_pair(i, 0) * oc_ref[:, sl] + gate_pair(i, 1) * os_ref[:, sl] + gate_pair(i, 2) * ow_ref[:, sl]
        acc = acc + _dot(oa.astype(BF16), w_ref[sl, :])
    acc = acc + _dot(ob_ref[...].astype(BF16), w_ref[n_a * LANE:, :])
    o_ref[...] = acc


def _even_out(y, oc, os_, ow, ob, misc, w, tm):
    m, d = y.shape
    row = lambda i: (i, 0)
    return pl.pallas_call(
        _even_out_kernel,
        grid=(m // tm,),
        in_specs=[pl.BlockSpec((tm, d), row)] + [pl.BlockSpec((tm, 512), row)] * 4
                 + [pl.BlockSpec((tm, LANE), row), pl.BlockSpec(w.shape, lambda i: (0, 0))],
        out_specs=pl.BlockSpec((tm, d), row),
        out_shape=jax.ShapeDtypeStruct((m, d), F32),
        compiler_params=_params(("parallel",)),
        name="even_out",
    )(y, oc, os_, ow, ob, misc, w)


def _odd_out_kernel(y_ref, o_ref_in, w_ref, o_ref):
    o_ref[...] = y_ref[...] + _dot(o_ref_in[...].astype(BF16), w_ref[...])


def _odd_out(y, o, w, tm):
    m, d = y.shape
    row = lambda i: (i, 0)
    return pl.pallas_call(
        _odd_out_kernel,
        grid=(m // tm,),
        in_specs=[pl.BlockSpec((tm, d), row), pl.BlockSpec((tm, o.shape[1]), row),
                  pl.BlockSpec(w.shape, lambda i: (0, 0))],
        out_specs=pl.BlockSpec((tm, d), row),
        out_shape=jax.ShapeDtypeStruct((m, d), F32),
        compiler_params=_params(("parallel",)),
        name="odd_out",
    )(y, o, w)


def _row_tile(m, cap=1024):
    t = cap
    while m % t:
        t //= 2
    return t


def _rope_tables(pos, half, seg_starts):
    inv = ROPE_THETA ** (-jnp.arange(half, dtype=F32) / half)
    ang = pos.astype(F32)[:, None] * inv[None, :]
    cos, sin = jnp.cos(ang), jnp.sin(ang)
    n = pos.shape[0]
    c = jnp.ones((n, LANE), F32)
    s1 = jnp.zeros((n, LANE), F32)
    s2 = jnp.zeros((n, LANE), F32)
    for st in seg_starts:
        c = c.at[:, st:st + half].set(cos).at[:, st + half:st + 2 * half].set(cos)
        s1 = s1.at[:, st:st + half].set(-sin)
        s2 = s2.at[:, st + half:st + 2 * half].set(sin)
    return c, s1, s2


def _tile_tables(tabs, reps):
    return tuple(jnp.tile(a, (reps, 1)) for a in tabs)


def _block_mean_matrix(group):
    i = np.arange(LANE)
    return jnp.asarray((i[:, None] // group == i[None, :] // group) / group, BF16)


def _even_weights(w_in, nsa_g, dsa_g):
    d = w_in.shape[0]
    n_q = NSA_KV_HEADS * NSA_GROUP * HEAD_DIM
    n_kv = 2 * NSA_KV_HEADS * HEAD_DIM
    splits = np.cumsum([n_q, 3 * n_kv, 3 * NSA_KV_HEADS * NSA_GROUP, n_q, n_kv, IDX_HEADS * IDX_DIM, IDX_HEADS])
    q_a, kv_a, gate, q_b, kv_b, qi, wi, ki = jnp.split(w_in, splits.tolist(), axis=1)
    pad = jnp.zeros((d, LANE - ki.shape[1] - gate.shape[1] - wi.shape[1]), w_in.dtype)
    w = jnp.concatenate([q_a, kv_a, q_b, kv_b, qi, ki, gate, wi, pad], axis=1).astype(BF16)
    one = jnp.ones((LANE,), F32)
    pair = lambda g: jnp.tile(g, 2)
    gain = jnp.concatenate(
        [pair(nsa_g[0])] * 4
        + [pair(nsa_g[1]), one, pair(nsa_g[2]), one, pair(nsa_g[3]), one]
        + [pair(dsa_g[0])] * 4 + [pair(dsa_g[1]), one] + [one] * 3).reshape(1, EVEN_TILES * LANE)
    return w, gain


def _compress_weights(pe, w1, w2):
    pe_t = jnp.concatenate([jnp.tile(pe[k], (1, NSA_KV_HEADS)) for k in range(2)], axis=1)
    hid = w1.shape[2]
    w1r = w1.reshape(2, CMP_BLOCK, HEAD_DIM, hid)
    z1 = jnp.zeros_like(w1r)
    w1bd = jnp.concatenate([jnp.concatenate([w1r, z1], axis=3), jnp.concatenate([z1, w1r], axis=3)], axis=2)
    z2 = jnp.zeros_like(w2)
    w2bd = jnp.concatenate([jnp.concatenate([w2, z2], axis=2), jnp.concatenate([z2, w2], axis=2)], axis=1)
    return pe_t, w1bd.astype(BF16), w2bd.astype(BF16)


def _pair_sum_matrix(nc, ns_pad):
    ratio = SEL_BLOCK // CMP_BLOCK
    return jnp.asarray(np.arange(nc)[:, None] // ratio == np.arange(ns_pad)[None, :], F32)


def _pad_heads(w, width):
    r, h, _ = w.shape
    return jnp.pad(w, ((0, 0), (0, 0), (0, LANE - width))).reshape(r, h * LANE)


def _pad_rows(x, n):
    return jnp.pad(x, ((0, 0), (0, n - x.shape[1]), (0, 0)))


def kernel(x_prompt, x_sample, cache_nsa_cmp_kv, cache_nsa_slc_kv, state_nsa_win_kv, cache_dsa_kv, cache_dsa_idx_k, cache_mla_latent, page_table, norm_g, ffn_w_gu, ffn_w_down, even_w_in, nsa_qk_g, nsa_cmp_pe, nsa_cmp_w1, nsa_cmp_w2, dsa_qk_g, even_w_out, odd_w_in, mla_norm_cq, mla_norm_ckv, mla_w_qb, mla_w_uk, mla_w_uv, mla_qk_g, odd_w_out):
    bp, s, d = x_prompt.shape
    bs, t, _ = x_sample.shape
    n_pages = page_table.shape[1]
    past = n_pages * PAGE_SIZE
    depth = norm_g.shape[0]
    n_pool = cache_nsa_cmp_kv.shape[1]
    assert s % CMP_BLOCK == 0 and s % LANE == 0 and t < CMP_BLOCK and t % SUBLANE == 0
    kv_w = 2 * NSA_KV_HEADS * HEAD_DIM
    lat_w = KV_LORA + MLA_ROPE
    yp = x_prompt.reshape(bp * s, d)
    ys = x_sample.reshape(bs * t, d)
    pt_flat = page_table.reshape(-1)
    tm_p = _row_tile(s)
    tm_s = _row_tile(bs * t, 512)
    tm_proj = _row_tile(s, 512)
    tm_odd = _row_tile(s, 256)
    pos_p = jnp.arange(s)
    pos_s = past + jnp.arange(t)
    tq = min(128, s)
    tk = min(512, s)
    pages_per_step = min(16, n_pages)

    nsa_tabs_p = _rope_tables(pos_p, ROT_DIM // 2, (0, HEAD_DIM))
    nsa_tabs_s = _tile_tables(_rope_tables(pos_s, ROT_DIM // 2, (0, HEAD_DIM)), tm_s // t)
    mla_tabs_p = _rope_tables(pos_p, MLA_ROPE // 2, (MLA_NOPE,))
    mla_tabs_s = _tile_tables(_rope_tables(pos_s, MLA_ROPE // 2, (MLA_NOPE,)), tm_s // t)
    bd64 = _block_mean_matrix(HEAD_DIM)
    ones96 = jnp.full((LANE, LANE), 1.0 / MLA_QK, BF16)

    def ffn(y, layer, i, gi, tm):
        return _ffn_half(y, norm_g[layer, gi], ffn_w_gu[layer, i].astype(BF16), ffn_w_down[layer, i].astype(BF16), tm)

    even_p = [[] for _ in range(5)]
    even_s = [[] for _ in range(5)]
    mla_p, mla_s = [], []
    for layer in range(depth):
        li = layer // 2
        yp = ffn(yp, layer, 0, 0, tm_p)
        ys = ffn(ys, layer, 0, 0, tm_s)
        if layer % 2 == 0:
            w_in, gain = _even_weights(even_w_in[li], nsa_qk_g[li], dsa_qk_g[li])
            pe_t, w1bd, w2bd = _compress_weights(nsa_cmp_pe[li], nsa_cmp_w1[li], nsa_cmp_w2[li])
            w_out = even_w_out[li].astype(BF16)
            g1 = norm_g[layer, 1]

            qa, kcmp, kslc, kwin, qb, kvb, qi, misc, ki = _even_project(
                yp, g1, w_in, gain, nsa_tabs_p, s // tm_proj, bd64, tm_proj)
            r3 = lambda a: a.reshape(bp, s, a.shape[1])
            nc = s // CMP_BLOCK
            kvc = _compress_dense(kcmp, pe_t, w1bd, w2bd, _row_tile(bp * s, 4096)).reshape(bp, nc, kv_w)
            ns_pad = -(-(s // SEL_BLOCK) // LANE) * LANE
            oc, mask_slc = _nsa_cmp_select(r3(qa), kvc, _pair_sum_matrix(nc, ns_pad), tq, 0, s, s)
            attn = functools.partial(_attention, n_grp=NSA_KV_HEADS, n_rep=NSA_GROUP, dk=HEAD_DIM, dv=HEAD_DIM,
                                     scale=HEAD_DIM ** -0.5, tq=tq, tk=tk, k_tile=0, v_tile=1)
            o_slc = attn(r3(qa), r3(kslc), r3(kslc), mask_slc, mode="array")
            o_win = attn(r3(qa), r3(kwin), r3(kwin), None, mode="window")
            mask_dsa = _dsa_index_dense(r3(qi), r3(misc), tq, min(DSA_TOPK, s // 4))
            o_dsa = attn(r3(qb), r3(kvb), r3(kvb), mask_dsa, mode="array")
            f2 = lambda a: a.reshape(bp * s, a.shape[2])
            yp = _even_out(yp, f2(oc), f2(o_slc), f2(o_win), f2(o_dsa), misc, w_out, tm_proj)
            rows6 = lambda a, n: a.reshape(n, -1, 2, NSA_KV_HEADS, HEAD_DIM)
            w_keep = min(WINDOW, s)
            for lst, a in zip(even_p, (rows6(kcmp, bp), rows6(kslc, bp), rows6(kwin, bp)[:, s - w_keep:],
                                       rows6(kvb, bp), ki.reshape(bp, s, IDX_DIM))):
                lst.append(a)

            qa, kcmp, kslc, kwin, qb, kvb, qi, misc, ki = _even_project(
                ys, g1, w_in, gain, nsa_tabs_s, 1, bd64, tm_s)
            r3 = lambda a: a.reshape(bs, t, a.shape[1])
            new_page = lambda a: _pad_rows(r3(a), PAGE_SIZE)
            cache2 = lambda c: c[li].reshape(n_pool, PAGE_SIZE, -1)
            nc = n_pages * (PAGE_SIZE // CMP_BLOCK)
            n_keys = past + t
            lk = (n_pages + 1) * PAGE_SIZE
            kvc = _compress_paged(cache2(cache_nsa_cmp_kv), pt_flat, bs, n_pages, pe_t, w1bd, w2bd,
                                  min(32, n_pages)).reshape(bs, nc, kv_w)
            ns_pad = -(-(lk // SEL_BLOCK) // LANE) * LANE
            oc, mask_slc = _nsa_cmp_select(r3(qa), kvc, _pair_sum_matrix(nc, ns_pad), t, past, n_keys, lk)
            paged = functools.partial(_paged_attention, pt_flat=pt_flat, n_pages=n_pages,
                                      n_per_step=pages_per_step, n_grp=NSA_KV_HEADS, n_rep=NSA_GROUP)
            o_slc = paged(r3(qa), mask_slc, new_page(kslc), cache2(cache_nsa_slc_kv))
            kv_win = jnp.concatenate([state_nsa_win_kv[li].reshape(bs, -1, kv_w), r3(kwin)], axis=1)
            wb = kv_win.shape[1] - t
            lw = -(-(wb + t) // LANE) * LANE
            dpos = (past + np.arange(t))[:, None] - (past - wb + np.arange(lw))[None, :]
            ok_w = (dpos >= 0) & (dpos <= WINDOW) & (np.arange(lw)[None, :] < wb + t) & ((past - wb + np.arange(lw))[None, :] >= 0)
            mask_win = jnp.asarray(ok_w[None, None], BF16)
            kv_win_pad = _pad_rows(kv_win, lw)
            o_win = _attention(r3(qa), kv_win_pad, kv_win_pad, mask_win, n_grp=NSA_KV_HEADS, n_rep=NSA_GROUP,
                               dk=HEAD_DIM, dv=HEAD_DIM, scale=HEAD_DIM ** -0.5, mode="array", tq=t, tk=lw,
                               k_tile=0, v_tile=1, skip=False)
            mask_dsa = _dsa_index_paged(r3(qi), r3(misc), new_page(misc), cache2(cache_dsa_idx_k), pt_flat,
                                        n_pages, min(32, n_pages), min(DSA_TOPK, n_keys // 4))
            o_dsa = paged(r3(qb), mask_dsa, new_page(kvb), cache2(cache_dsa_kv))
            f2 = lambda a: a.reshape(bs * t, a.shape[2])
            ys = _even_out(ys, f2(oc), f2(o_slc), f2(o_win), f2(o_dsa), misc, w_out, tm_s)
            w_keep = min(WINDOW, wb + t)
            for lst, a in zip(even_s, (rows6(kcmp, bs), rows6(kslc, bs), rows6(kv_win[:, wb + t - w_keep:], bs),
                                       rows6(kvb, bs), ki.reshape(bs, t, IDX_DIM))):
                lst.append(a)
        else:
            cq, ckv, kr = jnp.split(odd_w_in[li], [Q_LORA, Q_LORA + KV_LORA], axis=1)
            z = lambda n: jnp.zeros((d, n), odd_w_in.dtype)
            w_in = jnp.concatenate([cq, ckv, z(MLA_NOPE), kr, z(LANE - MLA_QK)], axis=1).astype(BF16)
            wqb = _pad_heads(mla_w_qb[li], MLA_QK).astype(BF16)
            wuk_pad = _pad_heads(mla_w_uk[li], MLA_NOPE).astype(BF16)
            wuk = mla_w_uk[li].reshape(KV_LORA, MLA_HEADS * MLA_NOPE).astype(BF16)
            wuv = mla_w_uv[li].reshape(KV_LORA, MLA_HEADS * MLA_V).astype(BF16)
            pad_gain = lambda g: jnp.tile(jnp.pad(g, (0, LANE - MLA_QK)), MLA_HEADS).reshape(1, MLA_HEADS * LANE)
            gq, gk = pad_gain(mla_qk_g[li, 0]), pad_gain(mla_qk_g[li, 1])
            ncq = mla_norm_cq[li].reshape(1, Q_LORA)
            nckv = mla_norm_ckv[li].reshape(1, KV_LORA)
            w_out = odd_w_out[li].astype(BF16)
            g1 = norm_g[layer, 1]

            q, lat, k, v = _odd_project(yp, g1, w_in, ncq, nckv, wqb, gq, mla_tabs_p, s // tm_odd, ones96,
                                        (wuk_pad, gk, wuv), tm_odd)
            r3 = lambda a: a.reshape(bp, s, a.shape[1])
            o = _attention(r3(q), r3(k), r3(v), None, n_grp=MLA_HEADS, n_rep=1, dk=MLA_PAD, dv=MLA_V,
                           scale=MLA_QK ** -0.5, mode="causal", tq=min(256, s), tk=tk)
            yp = _odd_out(yp, o.reshape(bp * s, -1), w_out, tm_p)
            mla_p.append(lat.reshape(bp, s, lat_w))

            q, lat = _odd_project(ys, g1, w_in, ncq, nckv, wqb, gq, mla_tabs_s, 1, ones96, None, tm_s)
            lat3 = lat.reshape(bs, t, lat_w)
            o = _mla_paged(q.reshape(bs, t, -1), gk, wuk, wuv, _pad_rows(lat3, PAGE_SIZE),
                           cache_mla_latent[li].reshape(n_pool, PAGE_SIZE, lat_w), pt_flat, n_pages,
                           min(8, n_pages))
            ys = _odd_out(ys, o.reshape(bs * t, -1), w_out, tm_s)
            mla_s.append(lat3)
        yp = ffn(yp, layer, 1, 2, tm_p)
        ys = ffn(ys, layer, 1, 2, tm_s)

    stack = lambda lst: jnp.stack(lst)
    cmp_p, slc_p, win_p, dsa_p, idx_p = [stack(a) for a in even_p]
    cmp_s, slc_s, win_s, dsa_s, idx_s = [stack(a) for a in even_s]
    return (yp.reshape(bp, s, d), ys.reshape(bs, t, d), cmp_p, cmp_s, slc_p, slc_s, win_p, win_s,
            dsa_p, dsa_s, idx_p, idx_s, stack(mla_p), stack(mla_s))
```

```python
import functools
import math

import numpy as np
import jax
import jax.numpy as jnp
from jax import lax
from jax.experimental import pallas as pl
from jax.experimental.pallas import tpu as pltpu

F32 = jnp.float32
BF16 = jnp.bfloat16

LANE = 128
SUBLANE = 8
MIB = 1024 * 1024

PAGE_SIZE = 128
HEAD_DIM = 64
ROT_DIM = HEAD_DIM // 4
ROPE_THETA = 500000.0
NORM_EPS = 1e-6
NEG_BIG = -1e30
MASK_BIAS = -2e30
NEG_INF_KEY = int(np.int32(np.array(-np.inf, np.float32).view(np.int32)) ^ np.int32(0x7FFFFFFF))
NSA_KV_HEADS = 2
NSA_GROUP = 4
CMP_BLOCK = 32
SEL_BLOCK = 64
N_SEL = 16
WINDOW = 512
FORCE_BONUS = float(NSA_GROUP + 1)
DSA_TOPK = 256
IDX_HEADS = 4
IDX_DIM = 64
MLA_HEADS = 16
MLA_NOPE = 64
MLA_ROPE = 32
MLA_V = 64
MLA_QK = MLA_NOPE + MLA_ROPE
MLA_PAD = LANE
Q_LORA = 256
KV_LORA = 256
GATE_LANE0 = IDX_DIM
WI_LANE0 = GATE_LANE0 + 3 * NSA_KV_HEADS * NSA_GROUP
HI = lax.Precision.HIGHEST


def _params(sem, vmem_mib=48):
    return pltpu.CompilerParams(dimension_semantics=sem, vmem_limit_bytes=vmem_mib * MIB)


def _dot(a, b, precision=None):
    return jnp.dot(a, b, preferred_element_type=F32, precision=precision)


def _dot_nt(a, b):
    return lax.dot_general(a, b, (((1,), (1,)), ((), ())), preferred_element_type=F32)


def _rms_rows(x, g):
    ms = jnp.mean(x * x, axis=-1, keepdims=True)
    return x * lax.rsqrt(ms + NORM_EPS) * g


def _group_mean_sq(x, ones_bf16):
    sq = x * x
    hi = sq.astype(BF16)
    lo = (sq - hi.astype(F32)).astype(BF16)
    return _dot(hi, ones_bf16) + _dot(lo, ones_bf16)


def _rope_lanes(x, c, s1, s2, half):
    return x * c + pltpu.roll(x, LANE - half, 1) * s1 + pltpu.roll(x, half, 1) * s2


def _ffn_kernel(x_ref, g_ref, wa_ref, wu_ref, wd_ref, o_ref, xn_ref, acc_ref):
    f = pl.program_id(1)

    @pl.when(f == 0)
    def _():
        xn_ref[...] = _rms_rows(x_ref[...], g_ref[...]).astype(BF16)
        acc_ref[...] = jnp.zeros_like(acc_ref)

    xn = xn_ref[...]
    a = _dot(xn, wa_ref[...])
    u = _dot(xn, wu_ref[...])
    h = (a * jax.nn.sigmoid(a) * u).astype(BF16)
    acc_ref[...] += _dot(h, wd_ref[...])

    @pl.when(f == pl.num_programs(1) - 1)
    def _():
        o_ref[...] = x_ref[...] + 0.5 * acc_ref[...]


def _ffn_half(x, g, w_gu, w_down, tm, tf=256):
    m, d = x.shape
    d_ff = w_down.shape[0]
    n_f = d_ff // tf
    return pl.pallas_call(
        _ffn_kernel,
        grid=(m // tm, n_f),
        in_specs=[
            pl.BlockSpec((tm, d), lambda i, f: (i, 0)),
            pl.BlockSpec((1, d), lambda i, f: (0, 0)),
            pl.BlockSpec((d, tf), lambda i, f: (0, f)),
            pl.BlockSpec((d, tf), lambda i, f: (0, n_f + f)),
            pl.BlockSpec((tf, d), lambda i, f: (f, 0)),
        ],
        out_specs=pl.BlockSpec((tm, d), lambda i, f: (i, 0)),
        out_shape=jax.ShapeDtypeStruct((m, d), F32),
        scratch_shapes=[pltpu.VMEM((tm, d), BF16), pltpu.VMEM((tm, d), F32)],
        compiler_params=_params(("parallel", "arbitrary")),
        name="ffn_half",
    )(x, g.reshape(1, d), w_gu, w_gu, w_down)


EVEN_TILES = 19


def _even_proj_kernel(x_ref, g_ref, w_ref, gain_ref, c_ref, s1_ref, s2_ref, bd_ref,
                      qa_ref, cmp_ref, slc_ref, win_ref, qb_ref, kvb_ref, qi_ref, misc_ref, ki_ref):
    xn = _rms_rows(x_ref[...], g_ref[...]).astype(BF16)
    proj = _dot(xn, w_ref[...])
    c, s1, s2 = c_ref[...], s1_ref[...], s2_ref[...]
    bd = bd_ref[...]
    half = ROT_DIM // 2

    def tile(i):
        return proj[:, i * LANE:(i + 1) * LANE]

    def normed(i):
        x = tile(i)
        return x * lax.rsqrt(_group_mean_sq(x, bd) + NORM_EPS) * gain_ref[:, i * LANE:(i + 1) * LANE]

    def rope(x):
        return _rope_lanes(x, c, s1, s2, half)

    for i in range(4):
        qa_ref[:, i * LANE:(i + 1) * LANE] = rope(normed(i))
    for br, ref in enumerate((cmp_ref, slc_ref, win_ref)):
        ref[:, :LANE] = rope(normed(4 + 2 * br))
        ref[:, LANE:] = tile(5 + 2 * br)
    for i in range(4):
        qb_ref[:, i * LANE:(i + 1) * LANE] = rope(normed(10 + i))
    kvb_ref[:, :LANE] = rope(normed(14))
    kvb_ref[:, LANE:] = tile(15)
    for i in range(2):
        qi_ref[:, i * LANE:(i + 1) * LANE] = rope(tile(16 + i))
    m = tile(18)
    mr = rope(m)
    lane = lax.broadcasted_iota(jnp.int32, m.shape, 1)
    misc = jnp.where(lane < GATE_LANE0, mr,
                     jnp.where(lane < WI_LANE0, jax.nn.sigmoid(m),
                               jnp.where(lane < WI_LANE0 + IDX_HEADS, m * IDX_HEADS ** -0.5, 0.0)))
    misc_ref[...] = misc
    ki_ref[...] = mr[:, :IDX_DIM]


def _even_project(x, g, w, gain, tabs, tab_period, bd, tm):
    m, d = x.shape
    c, s1, s2 = tabs
    row = lambda i: (i, 0)
    fixed = lambda i: (0, 0)
    tab = lambda i: (i % tab_period, 0)
    widths = (512, 256, 256, 256, 512, 256, 256, LANE, IDX_DIM)
    return pl.pallas_call(
        _even_proj_kernel,
        grid=(m // tm,),
        in_specs=[
            pl.BlockSpec((tm, d), row),
            pl.BlockSpec((1, d), fixed),
            pl.BlockSpec(w.shape, fixed),
            pl.BlockSpec(gain.shape, fixed),
            pl.BlockSpec((tm, LANE), tab), pl.BlockSpec((tm, LANE), tab), pl.BlockSpec((tm, LANE), tab),
            pl.BlockSpec(bd.shape, fixed),
        ],
        out_specs=[pl.BlockSpec((tm, wd), row) for wd in widths],
        out_shape=[jax.ShapeDtypeStruct((m, wd), F32) for wd in widths],
        compiler_params=_params(("parallel",)),
        name="even_project",
    )(x, g.reshape(1, d), w, gain, c, s1, s2, bd)


def _compress_rows(row_refs, n_blk, pe_ref, w1_ref, w2_ref):
    outs = []
    for k in range(2):
        acc = jnp.zeros((n_blk, 2 * LANE), F32)
        for j in range(CMP_BLOCK):
            xj = row_refs[k][pl.ds(j, n_blk, stride=CMP_BLOCK), :] + pe_ref[j:j + 1, k * LANE:(k + 1) * LANE]
            acc = acc + _dot(xj.astype(BF16), w1_ref[k, j])
        outs.append(_dot(jax.nn.gelu(acc).astype(BF16), w2_ref[k]))
    return jnp.concatenate(outs, axis=-1)


def _compress_kernel(k_ref, v_ref, pe_ref, w1_ref, w2_ref, o_ref):
    o_ref[...] = _compress_rows((k_ref, v_ref), o_ref.shape[0], pe_ref, w1_ref, w2_ref)


def _compress_dense(rows, pe, w1, w2, t_rows):
    m = rows.shape[0]
    n_blk = t_rows // CMP_BLOCK
    return pl.pallas_call(
        _compress_kernel,
        grid=(m // t_rows,),
        in_specs=[
            pl.BlockSpec((t_rows, LANE), lambda i: (i, 0)),
            pl.BlockSpec((t_rows, LANE), lambda i: (i, 1)),
            pl.BlockSpec(pe.shape, lambda i: (0, 0)),
            pl.BlockSpec(w1.shape, lambda i: (0, 0, 0, 0)),
            pl.BlockSpec(w2.shape, lambda i: (0, 0, 0)),
        ],
        out_specs=pl.BlockSpec((n_blk, 2 * LANE), lambda i: (i, 0)),
        out_shape=jax.ShapeDtypeStruct((m // CMP_BLOCK, 2 * LANE), F32),
        compiler_params=_params(("parallel",)),
        name="nsa_compress_dense",
    )(rows, rows, pe, w1, w2)


def _page_specs(n_per_step, n_pages, rows, cols):
    return [pl.BlockSpec((None, rows, cols),
                         lambda b, c, pt, p=p: (pt[b * n_pages + c * n_per_step + p], 0, 0))
            for p in range(n_per_step)]


def _compress_paged_kernel(pt_ref, pe_ref, w1_ref, w2_ref, *rest, n_per_step):
    pages = rest[:n_per_step]
    o_ref, kbuf_ref, vbuf_ref = rest[n_per_step:]
    for p in range(n_per_step):
        kbuf_ref[p * PAGE_SIZE:(p + 1) * PAGE_SIZE, :] = pages[p][:LANE, :].T
        vbuf_ref[p * PAGE_SIZE:(p + 1) * PAGE_SIZE, :] = pages[p][LANE:, :].T
    o_ref[...] = _compress_rows((kbuf_ref, vbuf_ref), o_ref.shape[0], pe_ref, w1_ref, w2_ref)


def _compress_paged(cache, pt_flat, n_seq, n_pages, pe, w1, w2, n_per_step):
    blk_per_step = n_per_step * PAGE_SIZE // CMP_BLOCK
    n_ch = n_pages // n_per_step
    grid_spec = pltpu.PrefetchScalarGridSpec(
        num_scalar_prefetch=1,
        grid=(n_seq, n_ch),
        in_specs=[
            pl.BlockSpec(pe.shape, lambda b, c, pt: (0, 0)),
            pl.BlockSpec(w1.shape, lambda b, c, pt: (0, 0, 0, 0)),
            pl.BlockSpec(w2.shape, lambda b, c, pt: (0, 0, 0)),
        ] + _page_specs(n_per_step, n_pages, 2 * LANE, PAGE_SIZE),
        out_specs=pl.BlockSpec((blk_per_step, 2 * LANE), lambda b, c, pt: (b * n_ch + c, 0)),
        scratch_shapes=[pltpu.VMEM((n_per_step * PAGE_SIZE, LANE), F32)] * 2,
    )
    return pl.pallas_call(
        functools.partial(_compress_paged_kernel, n_per_step=n_per_step),
        grid_spec=grid_spec,
        out_shape=jax.ShapeDtypeStruct((n_seq * n_pages * PAGE_SIZE // CMP_BLOCK, 2 * LANE), F32),
        compiler_params=_params(("parallel", "arbitrary")),
        name="nsa_compress_paged",
    )(pt_flat, pe, w1, w2, *([cache] * n_per_step))


def _topk_mask(score, idx, k, idx_bits):
    def count(pred):
        v = jnp.where(pred, 1.0, 0.0)
        if v.ndim == 3:
            v = jnp.sum(v, axis=0)
        return jnp.sum(v, axis=-1, keepdims=True)

    rows = score.shape[-2]
    bits = pltpu.bitcast(score + 0.0, jnp.int32)
    key = jnp.where(bits < 0, bits ^ jnp.int32(0x7FFFFFFF), bits)
    kf = float(k)
    int_min = jnp.full((rows, 1), -2 ** 31, jnp.int32)
    t0 = jnp.where(count(key >= 0) >= kf, jnp.zeros((rows, 1), jnp.int32), int_min)

    def value_bit(i, t):
        cand = t | jnp.left_shift(jnp.int32(1), 30 - i)
        return jnp.where(count(key >= cand) >= kf, cand, t)

    thr = lax.fori_loop(0, 31, value_bit, t0)
    need = kf - count(key > thr)
    tie_idx = jnp.where(key == thr, idx, jnp.int32(2 ** 30))

    def index_bit(i, j):
        cand = j | jnp.left_shift(jnp.int32(1), idx_bits - 1 - i)
        return jnp.where(count(tie_idx < cand) <= need, cand, j)

    j_cut = lax.fori_loop(0, idx_bits, index_bit, jnp.zeros((rows, 1), jnp.int32))
    return jnp.where(key > thr, 1.0, jnp.where(tie_idx < j_cut, 1.0, 0.0))


def _nsa_cmp_kernel(q_ref, kvc_ref, pm_ref, oc_ref, mask_ref, *, tq, pos0, n_keys, lk, chunk_major):
    nc = kvc_ref.shape[1]
    nsp = pm_ref.shape[1]
    ns = -(-n_keys // SEL_BLOCK)
    n_sel = min(N_SEL, ns)
    r_rows = NSA_GROUP * tq
    q = q_ref[0]
    kvc = kvc_ref[0]
    pos_t = pos0 + pl.program_id(1) * tq + lax.broadcasted_iota(jnp.int32, (tq, 1), 0)
    pos_r = jnp.concatenate([pos_t] * NSA_GROUP, axis=0)
    c_end = (lax.broadcasted_iota(jnp.int32, (r_rows, nc), 1) + 1) * CMP_BLOCK - 1
    vis = c_end <= pos_r
    blk = lax.broadcasted_iota(jnp.int32, (tq, nsp), 1)
    cur = pos_t // SEL_BLOCK
    forced = jnp.where(blk == 0, FORCE_BONUS, jnp.where(blk == cur, FORCE_BONUS, jnp.where(blk == cur - 1, FORCE_BONUS, 0.0)))
    lane = lax.broadcasted_iota(jnp.int32, (tq, LANE), 1)
    for g in range(NSA_KV_HEADS):
        qg = jnp.concatenate([q[:, (g * NSA_GROUP + r) * HEAD_DIM:(g * NSA_GROUP + r + 1) * HEAD_DIM]
                              for r in range(NSA_GROUP)], axis=0).astype(BF16)
        kg = kvc[:, g * HEAD_DIM:(g + 1) * HEAD_DIM].astype(BF16)
        vg = kvc[:, LANE + g * HEAD_DIM:LANE + (g + 1) * HEAD_DIM].astype(BF16)
        s = jnp.where(vis, _dot_nt(qg, kg) * HEAD_DIM ** -0.5, -jnp.inf)
        mx = jnp.max(s, axis=-1, keepdims=True)
        mx = jnp.where(mx > -jnp.inf, mx, 0.0)
        e = jnp.where(vis, jnp.exp(s - mx), 0.0)
        p = e / jnp.maximum(jnp.sum(e, axis=-1, keepdims=True), 1e-30)
        o = _dot(p.astype(BF16), vg)
        for r in range(NSA_GROUP):
            h = g * NSA_GROUP + r
            oc_ref[0, :, h * HEAD_DIM:(h + 1) * HEAD_DIM] = o[r * tq:(r + 1) * tq]
        imp = p[0:tq]
        for r in range(1, NSA_GROUP):
            imp = imp + p[r * tq:(r + 1) * tq]
        imp = _dot(imp, pm_ref[...], precision=HI)
        score = jnp.where(blk * SEL_BLOCK <= pos_t, imp + forced, -1.0)
        score = jnp.where(blk < ns, score, -jnp.inf)
        sel = _topk_mask(score, blk, n_sel, nsp.bit_length())
        sel = jnp.where(score >= 0.0, sel, 0.0)
        for c in range(lk // LANE):
            lo = jnp.broadcast_to(sel[:, 2 * c:2 * c + 1], (tq, LANE))
            hi = jnp.broadcast_to(sel[:, 2 * c + 1:2 * c + 2], (tq, LANE))
            chunk = jnp.where(lane < SEL_BLOCK, lo, hi)
            chunk = jnp.where(c * LANE + lane <= pos_t, chunk, 0.0)
            bias = jnp.where(chunk > 0.5, 0.0, MASK_BIAS).astype(mask_ref.dtype)
            if chunk_major:
                mask_ref[0, g, c] = bias
            else:
                mask_ref[0, g, :, c * LANE:(c + 1) * LANE] = bias


def _nsa_cmp_select(q, kvc, pm, tq, pos0, n_keys, lk, chunk_major):
    b, sq, _ = q.shape
    nc = kvc.shape[1]
    if chunk_major:
        mask_spec = pl.BlockSpec((1, NSA_KV_HEADS, lk // LANE, tq, LANE), lambda i, j: (i, 0, 0, j, 0))
        mask_shape = jax.ShapeDtypeStruct((b, NSA_KV_HEADS, lk // LANE, sq, LANE), F32)
    else:
        mask_spec = pl.BlockSpec((1, NSA_KV_HEADS, tq, lk), lambda i, j: (i, 0, j, 0))
        mask_shape = jax.ShapeDtypeStruct((b, NSA_KV_HEADS, sq, lk), BF16)
    return pl.pallas_call(
        functools.partial(_nsa_cmp_kernel, tq=tq, pos0=pos0, n_keys=n_keys, lk=lk, chunk_major=chunk_major),
        grid=(b, sq // tq),
        in_specs=[
            pl.BlockSpec((1, tq, 512), lambda i, j: (i, j, 0)),
            pl.BlockSpec((1, nc, 2 * LANE), lambda i, j: (i, 0, 0)),
            pl.BlockSpec(pm.shape, lambda i, j: (0, 0)),
        ],
        out_specs=[pl.BlockSpec((1, tq, 512), lambda i, j: (i, j, 0)), mask_spec],
        out_shape=[jax.ShapeDtypeStruct((b, sq, 512), F32), mask_shape],
        compiler_params=_params(("parallel", "parallel")),
        name="nsa_cmp_select",
    )(q, kvc, pm)


def _idx_scores(qi, wi, ki_t):
    out = None
    for h in range(IDX_HEADS):
        lg = _dot(qi[:, h * IDX_DIM:(h + 1) * IDX_DIM].astype(BF16), ki_t) * IDX_DIM ** -0.5
        term = wi[:, WI_LANE0 + h:WI_LANE0 + h + 1] * jnp.maximum(lg, 0.0)
        out = term if out is None else out + term
    return out


def _sort_key(score):
    bits = pltpu.bitcast(score + 0.0, jnp.int32)
    return jnp.where(bits < 0, bits ^ jnp.int32(0x7FFFFFFF), bits)


def _topk_bias_chunks(key_ref, n_c, k, idx_bits, write_chunk):
    _, rows, w = key_ref.shape
    kf = float(k)
    neg_inf_key = jnp.int32(NEG_INF_KEY)

    def lane_sum(v):
        out = v[:, 0:LANE]
        for i in range(1, w // LANE):
            out = out + v[:, i * LANE:(i + 1) * LANE]
        return out

    def count(pred_of_chunk):
        def body(c, acc):
            return acc + lane_sum(jnp.where(pred_of_chunk(c, key_ref[c]), 1.0, 0.0))
        acc = lax.fori_loop(0, n_c, body, jnp.zeros((rows, LANE), F32))
        return jnp.sum(acc, axis=-1, keepdims=True)

    def count_ge(t):
        tb = jnp.broadcast_to(t, (rows, w))
        return count(lambda c, key: key >= tb)

    int_min = jnp.full((rows, 1), -2 ** 31, jnp.int32)
    t0 = jnp.where(count_ge(jnp.zeros((rows, 1), jnp.int32)) >= kf, jnp.zeros((rows, 1), jnp.int32), int_min)

    def value_bit(i, carry):
        t, n_ge = carry
        cand = t | jnp.left_shift(jnp.int32(1), 30 - i)
        n = count_ge(cand)
        keep = n >= kf
        return jnp.where(keep, cand, t), jnp.where(keep, n, n_ge)

    thr, n_ge = lax.fori_loop(0, 31, value_bit, (t0, count_ge(t0)))
    lane = lax.broadcasted_iota(jnp.int32, (rows, w), 1)

    def tie_cut():
        thr_b = jnp.broadcast_to(thr, (rows, w))
        need = kf - count(lambda c, key: key > thr_b)

        def index_bit(i, j):
            cand = j | jnp.left_shift(jnp.int32(1), idx_bits - 1 - i)
            cb = jnp.broadcast_to(cand, (rows, w))
            n = count(lambda c, key: jnp.where(key == thr_b, c * w + lane, jnp.int32(2 ** 30)) < cb)
            return jnp.where(n <= need, cand, j)

        return lax.fori_loop(0, idx_bits, index_bit, jnp.zeros((rows, 1), jnp.int32))

    j_cut = lax.cond(jnp.max(n_ge) > kf, tie_cut, lambda: jnp.full((rows, 1), 2 ** 30, jnp.int32))
    thr_b = jnp.broadcast_to(thr, (rows, w))
    cut_b = jnp.broadcast_to(j_cut, (rows, w))

    def emit(c, _):
        key = key_ref[c]
        tie_ok = jnp.where(key == thr_b, c * w + lane, jnp.int32(2 ** 30)) < cut_b
        sel = jnp.where(key > thr_b, 0.0, jnp.where(tie_ok, 0.0, MASK_BIAS))
        write_chunk(c, jnp.where(key > neg_inf_key, sel, MASK_BIAS))
        return 0

    lax.fori_loop(0, n_c, emit, 0)


def _dsa_index_kernel(qi_ref, mq_ref, kit_ref, bias_ref, key_ref, *, tq, w, k_top):
    sk = kit_ref.shape[2]
    n_w = sk // w
    q_lo = pl.program_id(1) * tq
    n_c = (q_lo + tq + w - 1) // w
    qi, mq = qi_ref[0], mq_ref[0]
    pos_q = q_lo + lax.broadcasted_iota(jnp.int32, (tq, 1), 0)
    lane = lax.broadcasted_iota(jnp.int32, (tq, w), 1)
    for c in range(n_w):
        @pl.when(c < n_c)
        def _(c=c):
            score = _idx_scores(qi, mq, kit_ref[0, :, c * w:(c + 1) * w].astype(BF16))
            key_ref[c] = _sort_key(jnp.where(c * w + lane <= pos_q, score, -jnp.inf))

        @pl.when(c >= n_c)
        def _(c=c):
            bias_ref[0, 0, :, c * w:(c + 1) * w] = jnp.full((tq, w), MASK_BIAS, BF16)

    def write_chunk(c, bias):
        for cc in range(n_w):
            @pl.when(c == cc)
            def _(cc=cc):
                bias_ref[0, 0, :, cc * w:(cc + 1) * w] = bias.astype(BF16)

    _topk_bias_chunks(key_ref, n_c, k_top, sk.bit_length(), write_chunk)


def _dsa_index_dense(qi, misc, ki_t, tq, k_top, w=512):
    b, s, _ = qi.shape
    w = min(w, s)
    return pl.pallas_call(
        functools.partial(_dsa_index_kernel, tq=tq, w=w, k_top=k_top),
        grid=(b, s // tq),
        in_specs=[
            pl.BlockSpec((1, tq, IDX_HEADS * IDX_DIM), lambda i, j: (i, j, 0)),
            pl.BlockSpec((1, tq, LANE), lambda i, j: (i, j, 0)),
            pl.BlockSpec((1, IDX_DIM, s), lambda i, j: (i, 0, 0)),
        ],
        out_specs=pl.BlockSpec((1, 1, tq, s), lambda i, j: (i, 0, j, 0)),
        out_shape=jax.ShapeDtypeStruct((b, 1, s, s), BF16),
        scratch_shapes=[pltpu.VMEM((s // w, tq, w), jnp.int32)],
        compiler_params=_params(("parallel", "parallel")),
        name="dsa_index_dense",
    )(qi, misc, ki_t)


def _dsa_index_paged_kernel(pt_ref, qi_ref, mq_ref, new_ref, *rest, n_per_step, n_pages, t_new, k_top):
    pages = rest[:n_per_step]
    bias_ref, key_ref = rest[n_per_step], rest[n_per_step + 1]
    c = pl.program_id(1)
    n_tiles = n_pages + 1
    qi, mq = qi_ref[0], mq_ref[0]
    for p in range(n_per_step):
        key_ref[c * n_per_step + p] = _sort_key(_idx_scores(qi, mq, pages[p][...].astype(BF16)))

    @pl.when(c == pl.num_programs(1) - 1)
    def _():
        s_new = _idx_scores(qi, mq, new_ref[0].astype(BF16))
        row = lax.broadcasted_iota(jnp.int32, (t_new, LANE), 0)
        col = lax.broadcasted_iota(jnp.int32, (t_new, LANE), 1)
        key_ref[n_pages] = _sort_key(jnp.where(col <= row, s_new, -jnp.inf))

        def write_chunk(i, bias):
            bias_ref[0, 0, pl.ds(i, 1)] = bias.astype(BF16)[None]

        _topk_bias_chunks(key_ref, n_tiles, k_top, (n_tiles * LANE).bit_length(), write_chunk)


def _dsa_index_paged(qi, misc, ki_new_t, cache_idx_t, pt_flat, n_pages, n_per_step, k_top):
    b, t, _ = qi.shape
    grid_spec = pltpu.PrefetchScalarGridSpec(
        num_scalar_prefetch=1,
        grid=(b, n_pages // n_per_step),
        in_specs=[
            pl.BlockSpec((1, t, IDX_HEADS * IDX_DIM), lambda i, c, pt: (i, 0, 0)),
            pl.BlockSpec((1, t, LANE), lambda i, c, pt: (i, 0, 0)),
            pl.BlockSpec((1, IDX_DIM, PAGE_SIZE), lambda i, c, pt: (i, 0, 0)),
        ] + _page_specs(n_per_step, n_pages, IDX_DIM, PAGE_SIZE),
        out_specs=pl.BlockSpec((1, 1, n_pages + 1, t, LANE), lambda i, c, pt: (i, 0, 0, 0, 0)),
        scratch_shapes=[pltpu.VMEM((n_pages + 1, t, LANE), jnp.int32)],
    )
    return pl.pallas_call(
        functools.partial(_dsa_index_paged_kernel, n_per_step=n_per_step, n_pages=n_pages, t_new=t, k_top=k_top),
        grid_spec=grid_spec,
        out_shape=jax.ShapeDtypeStruct((b, 1, n_pages + 1, t, LANE), BF16),
        compiler_params=_params(("parallel", "arbitrary")),
        name="dsa_index_paged",
    )(pt_flat, qi, misc, ki_new_t, *([cache_idx_t] * n_per_step))


def _stack_heads(q, g, n_rep, dk):
    return jnp.concatenate([q[:, (g * n_rep + r) * dk:(g * n_rep + r + 1) * dk] for r in range(n_rep)],
                           axis=0).astype(BF16)


def _online_update(s, g, m_ref, l_ref, acc_ref, pv):
    m_prev = m_ref[g]
    m_new = jnp.maximum(m_prev, jnp.max(s, axis=-1, keepdims=True))
    p = jnp.exp(s - m_new)
    corr = jnp.exp(m_prev - m_new)
    l_ref[g] = l_ref[g] * corr + jnp.sum(p, axis=-1, keepdims=True)
    acc_ref[g] = acc_ref[g] * corr + pv(p.astype(BF16))
    m_ref[g] = m_new


def _attn_init(m_ref, l_ref, acc_ref):
    m_ref[...] = jnp.full(m_ref.shape, NEG_BIG, F32)
    l_ref[...] = jnp.zeros_like(l_ref)
    acc_ref[...] = jnp.zeros_like(acc_ref)


def _attn_finish(o_ref, l_ref, acc_ref, n_grp, n_rep, tq, dv):
    for g in range(n_grp):
        o = acc_ref[g] / jnp.maximum(l_ref[g], 1e-30)
        for r in range(n_rep):
            h = g * n_rep + r
            o_ref[0, :, h * dv:(h + 1) * dv] = o[r * tq:(r + 1) * tq]


def _tile_rows(x, n_rep):
    return x if n_rep == 1 else jnp.concatenate([x] * n_rep, axis=0)


def _attn_kernel(*refs, n_grp, n_rep, dk, dv, scale, mode, tq, tk, window, mask_groups, skip):
    if mode == "array":
        q_ref, k_ref, v_ref, mask_ref, o_ref, m_ref, l_ref, acc_ref = refs
    else:
        q_ref, k_ref, v_ref, o_ref, m_ref, l_ref, acc_ref = refs
        mask_ref = None
    qi, kc = pl.program_id(1), pl.program_id(2)

    @pl.when(kc == 0)
    def _():
        _attn_init(m_ref, l_ref, acc_ref)

    q_lo = qi * tq
    relevant = kc * tk <= q_lo + tq - 1 if skip else kc >= 0
    if mode == "window":
        relevant = jnp.logical_and(relevant, kc * tk + tk - 1 >= q_lo - window)

    @pl.when(relevant)
    def _():
        q, k, v = q_ref[0], k_ref[0], v_ref[0]
        q_scale, s_scale = (scale, None) if math.frexp(scale)[0] == 0.5 else (None, scale)
        if q_scale is not None:
            q = q * q_scale
        if mode != "array":
            pos_q = _tile_rows(q_lo + lax.broadcasted_iota(jnp.int32, (tq, 1), 0), n_rep)
            dpos = pos_q - (kc * tk + lax.broadcasted_iota(jnp.int32, (n_rep * tq, tk), 1))
            ok_pos = dpos >= 0 if mode == "causal" else jnp.abs(2 * dpos - window) <= window
        for g in range(n_grp):
            qg = _stack_heads(q, g, n_rep, dk)
            kg = k[:, g * dk:(g + 1) * dk].astype(BF16)
            vg = v[:, g * dv:(g + 1) * dv].astype(BF16)
            s = _dot_nt(qg, kg)
            if s_scale is not None:
                s = s * s_scale
            if mode == "array":
                s = s + _tile_rows(mask_ref[0, g if mask_groups > 1 else 0].astype(F32), n_rep)
            else:
                s = jnp.where(ok_pos, s, MASK_BIAS)
            _online_update(s, g, m_ref, l_ref, acc_ref, lambda pb, vg=vg: _dot(pb, vg))

    @pl.when(kc == pl.num_programs(2) - 1)
    def _():
        _attn_finish(o_ref, l_ref, acc_ref, n_grp, n_rep, tq, dv)


def _attention(q, k, v, mask, *, n_grp, n_rep, dk, dv, scale, mode, tq, tk, k_tile=0, v_tile=0, skip=True):
    b, sq, _ = q.shape
    sk = k.shape[1]
    n_kc = sk // tk

    def kc_eff(j, c):
        if not skip:
            return c
        hi = (j * tq + tq - 1) // tk
        c = jnp.minimum(c, hi)
        if mode == "window":
            c = jnp.maximum(c, jnp.maximum(j * tq - WINDOW, 0) // tk)
        return c

    in_specs = [
        pl.BlockSpec((1, tq, n_grp * n_rep * dk), lambda i, j, c: (i, j, 0)),
        pl.BlockSpec((1, tk, n_grp * dk), lambda i, j, c: (i, kc_eff(j, c), k_tile)),
        pl.BlockSpec((1, tk, n_grp * dv), lambda i, j, c: (i, kc_eff(j, c), v_tile)),
    ]
    args = [q, k, v]
    mask_groups = 1
    if mode == "array":
        mb, mask_groups = mask.shape[0], mask.shape[1]
        in_specs.append(pl.BlockSpec((1, mask_groups, tq, tk),
                                     lambda i, j, c: (i if mb > 1 else 0, 0, j, kc_eff(j, c))))
        args.append(mask)
    rows = n_rep * tq
    kern = functools.partial(_attn_kernel, n_grp=n_grp, n_rep=n_rep, dk=dk, dv=dv, scale=scale, mode=mode,
                             tq=tq, tk=tk, window=WINDOW, mask_groups=mask_groups, skip=skip)
    return pl.pallas_call(
        kern,
        grid=(b, sq // tq, n_kc),
        in_specs=in_specs,
        out_specs=pl.BlockSpec((1, tq, n_grp * n_rep * dv), lambda i, j, c: (i, j, 0)),
        out_shape=jax.ShapeDtypeStruct((b, sq, n_grp * n_rep * dv), F32),
        scratch_shapes=[pltpu.VMEM((n_grp, rows, 1), F32), pltpu.VMEM((n_grp, rows, 1), F32),
                        pltpu.VMEM((n_grp, rows, dv), F32)],
        compiler_params=_params(("parallel", "parallel", "arbitrary")),
        name="attention_" + mode,
    )(*args)


def _paged_attn_kernel(pt_ref, q_ref, bias_ref, bias_new_ref, new_ref, *rest, n_per_step, n_grp, n_rep, mask_groups):
    pages = rest[:n_per_step]
    o_ref, m_ref, l_ref, acc_ref = rest[n_per_step:]
    c = pl.program_id(1)
    t = q_ref.shape[1]

    @pl.when(c == 0)
    def _():
        _attn_init(m_ref, l_ref, acc_ref)

    q = q_ref[0] * HEAD_DIM ** -0.5

    def k_of(page_ref, g):
        return page_ref[g * HEAD_DIM:(g + 1) * HEAD_DIM, :].astype(BF16)

    def v_of(page_ref, g):
        return page_ref[LANE + g * HEAD_DIM:LANE + (g + 1) * HEAD_DIM, :].astype(BF16)

    def update(page_refs, bias_of):
        n = len(page_refs)
        for g in range(n_grp):
            gm = g if mask_groups > 1 else 0
            qg = _stack_heads(q, g, n_rep, HEAD_DIM)
            s = jnp.concatenate([_dot(qg, k_of(page_refs[p], g)) + _tile_rows(bias_of(gm, p), n_rep)
                                 for p in range(n)], axis=1)

            def pv(pb, g=g):
                out = None
                for p in range(n):
                    term = _dot_nt(pb[:, p * PAGE_SIZE:(p + 1) * PAGE_SIZE], v_of(page_refs[p], g))
                    out = term if out is None else out + term
                return out

            _online_update(s, g, m_ref, l_ref, acc_ref, pv)

    update(pages, lambda gm, p: bias_ref[0, gm, p])

    @pl.when(c == pl.num_programs(1) - 1)
    def _():
        update([new_ref.at[0]], lambda gm, p: bias_new_ref[0, gm, 0])
        _attn_finish(o_ref, l_ref, acc_ref, n_grp, n_rep, t, HEAD_DIM)


def _paged_attention(q, bias, new_t, cache_t, pt_flat, n_pages, n_per_step, n_grp, n_rep):
    b, t, hd = q.shape
    mask_groups = bias.shape[1]
    rows = n_rep * t
    grid_spec = pltpu.PrefetchScalarGridSpec(
        num_scalar_prefetch=1,
        grid=(b, n_pages // n_per_step),
        in_specs=[
            pl.BlockSpec((1, t, hd), lambda i, c, pt: (i, 0, 0)),
            pl.BlockSpec((1, mask_groups, n_per_step, t, LANE), lambda i, c, pt: (i, 0, c, 0, 0)),
            pl.BlockSpec((1, mask_groups, 1, t, LANE), lambda i, c, pt: (i, 0, n_pages, 0, 0)),
            pl.BlockSpec((1, 2 * LANE, PAGE_SIZE), lambda i, c, pt: (i, 0, 0)),
        ] + _page_specs(n_per_step, n_pages, 2 * LANE, PAGE_SIZE),
        out_specs=pl.BlockSpec((1, t, hd), lambda i, c, pt: (i, 0, 0)),
        scratch_shapes=[pltpu.VMEM((n_grp, rows, 1), F32), pltpu.VMEM((n_grp, rows, 1), F32),
                        pltpu.VMEM((n_grp, rows, HEAD_DIM), F32)],
    )
    return pl.pallas_call(
        functools.partial(_paged_attn_kernel, n_per_step=n_per_step, n_grp=n_grp, n_rep=n_rep,
                          mask_groups=mask_groups),
        grid_spec=grid_spec,
        out_shape=jax.ShapeDtypeStruct((b, t, hd), F32),
        compiler_params=_params(("parallel", "arbitrary")),
        name="paged_attention",
    )(pt_flat, q, bias, bias, new_t, *([cache_t] * n_per_step))


def _odd_proj_kernel(*refs, with_kv):
    if with_kv:
        (x_ref, g_ref, w_ref, ncq_ref, nckv_ref, wqb_ref, gq_ref, c_ref, s1_ref, s2_ref, ones_ref,
         wuk_ref, gk_ref, wuv_ref, q_ref, lat_ref, k_ref, v_ref) = refs
    else:
        (x_ref, g_ref, w_ref, ncq_ref, nckv_ref, wqb_ref, gq_ref, c_ref, s1_ref, s2_ref, ones_ref,
         q_ref, lat_ref) = refs
    xn = _rms_rows(x_ref[...], g_ref[...]).astype(BF16)
    proj = _dot(xn, w_ref[...])
    c, s1, s2 = c_ref[...], s1_ref[...], s2_ref[...]
    ones = ones_ref[...]
    half = MLA_ROPE // 2

    def head_norm(x, gain):
        return x * lax.rsqrt(_group_mean_sq(x, ones) + NORM_EPS) * gain

    cqn = _rms_rows(proj[:, :Q_LORA], ncq_ref[...]).astype(BF16)
    qp = _dot(cqn, wqb_ref[...])
    c_lat = _rms_rows(proj[:, Q_LORA:Q_LORA + KV_LORA], nckv_ref[...])
    kr = _rope_lanes(proj[:, Q_LORA + KV_LORA:], c, s1, s2, half)
    lat_ref[:, :KV_LORA] = c_lat
    lat_ref[:, KV_LORA:] = kr[:, MLA_NOPE:MLA_QK]
    for h in range(MLA_HEADS):
        sl = slice(h * MLA_PAD, (h + 1) * MLA_PAD)
        q_ref[:, sl] = head_norm(_rope_lanes(qp[:, sl], c, s1, s2, half), gq_ref[:, sl]).astype(q_ref.dtype)
    if with_kv:
        cb = c_lat.astype(BF16)
        kn = _dot(cb, wuk_ref[...])
        for h in range(MLA_HEADS):
            sl = slice(h * MLA_PAD, (h + 1) * MLA_PAD)
            k_ref[:, sl] = head_norm(kn[:, sl] + kr, gk_ref[:, sl]).astype(k_ref.dtype)
        v_ref[...] = _dot(cb, wuv_ref[...]).astype(v_ref.dtype)


def _odd_project(x, g, w, ncq, nckv, wqb, gq, tabs, tab_period, ones, kv_weights, tm):
    m, d = x.shape
    c, s1, s2 = tabs
    row = lambda i: (i, 0)
    fixed = lambda i: (0, 0)
    tab = lambda i: (i % tab_period, 0)
    with_kv = kv_weights is not None
    args = [x, g.reshape(1, d), w, ncq, nckv, wqb, gq, c, s1, s2, ones]
    in_specs = [pl.BlockSpec((tm, d), row), pl.BlockSpec((1, d), fixed), pl.BlockSpec(w.shape, fixed),
                pl.BlockSpec(ncq.shape, fixed), pl.BlockSpec(nckv.shape, fixed), pl.BlockSpec(wqb.shape, fixed),
                pl.BlockSpec(gq.shape, fixed),
                pl.BlockSpec((tm, LANE), tab), pl.BlockSpec((tm, LANE), tab), pl.BlockSpec((tm, LANE), tab),
                pl.BlockSpec(ones.shape, fixed)]
    widths = [MLA_HEADS * MLA_PAD, KV_LORA + MLA_ROPE]
    if with_kv:
        args += list(kv_weights)
        in_specs += [pl.BlockSpec(a.shape, fixed) for a in kv_weights]
        widths += [MLA_HEADS * MLA_PAD, MLA_HEADS * MLA_V]
    dtypes = [BF16, F32, BF16, BF16] if with_kv else [F32, F32]
    return pl.pallas_call(
        functools.partial(_odd_proj_kernel, with_kv=with_kv),
        grid=(m // tm,),
        in_specs=in_specs,
        out_specs=[pl.BlockSpec((tm, wd), row) for wd in widths],
        out_shape=[jax.ShapeDtypeStruct((m, wd), dt) for wd, dt in zip(widths, dtypes)],
        compiler_params=_params(("parallel",)),
        name="odd_project",
    )(*args)


def _mla_paged_kernel(pt_ref, q_ref, gk_ref, wuk_ref, wuv_ref, new_ref, *rest, n_per_step):
    pages = rest[:n_per_step]
    o_ref, buf_ref, qbd_ref, qr_ref, m_ref, l_ref, acc_ref = rest[n_per_step:]
    c = pl.program_id(1)
    t = q_ref.shape[1]
    rows = MLA_HEADS * t
    d_nope = MLA_HEADS * MLA_NOPE
    scale = MLA_QK ** -0.5

    def head_diag(shape):
        return (lax.broadcasted_iota(jnp.int32, shape, 0) // t
                == lax.broadcasted_iota(jnp.int32, shape, 1) // MLA_NOPE)

    @pl.when(c == 0)
    def _():
        qg = q_ref[0] * gk_ref[...]
        nope = jnp.concatenate([qg[:, h * MLA_PAD:h * MLA_PAD + MLA_NOPE] for h in range(MLA_HEADS)], axis=1)
        diag = head_diag((rows, d_nope))
        qbd_ref[...] = jnp.where(diag, _tile_rows(nope, MLA_HEADS), 0.0).astype(BF16)
        qr_ref[...] = jnp.concatenate([qg[:, h * MLA_PAD + MLA_NOPE:h * MLA_PAD + MLA_QK]
                                       for h in range(MLA_HEADS)], axis=0).astype(BF16)
        m_ref[...] = jnp.full(m_ref.shape, NEG_BIG, F32)
        l_ref[...] = jnp.zeros_like(l_ref)
        acc_ref[...] = jnp.zeros_like(acc_ref)

    def process(lat_t, ok):
        n = lat_t.shape[1]
        cb = lat_t[:KV_LORA].astype(BF16)
        kr = lat_t[KV_LORA:]
        kn = _dot(wuk_ref[...], cb)
        s = _dot(qbd_ref[...], kn.astype(BF16)) + _dot(qr_ref[...], kr.astype(BF16))
        ss = (jnp.sum((kn * kn).reshape(MLA_HEADS, MLA_NOPE, n), axis=1)
              + jnp.sum(kr * kr, axis=0, keepdims=True))
        r = lax.rsqrt(ss * (1.0 / MLA_QK) + NORM_EPS) * scale
        s = s * jnp.broadcast_to(r[:, None, :], (MLA_HEADS, t, n)).reshape(rows, n)
        if ok is not None:
            s = jnp.where(ok, s, MASK_BIAS)
        m_prev = m_ref[...]
        m_new = jnp.maximum(m_prev, jnp.max(s, axis=-1, keepdims=True))
        p = jnp.exp(s - m_new)
        corr = jnp.exp(m_prev - m_new)
        l_ref[...] = l_ref[...] * corr + jnp.sum(p, axis=-1, keepdims=True)
        acc_ref[...] = acc_ref[...] * corr + _dot_nt(p.astype(BF16), cb)
        m_ref[...] = m_new

    for p in range(n_per_step):
        buf_ref[:, p * PAGE_SIZE:(p + 1) * PAGE_SIZE] = pages[p][...]
    process(buf_ref[...], None)

    @pl.when(c == pl.num_programs(1) - 1)
    def _():
        q_t = lax.broadcasted_iota(jnp.int32, (rows, PAGE_SIZE), 0) % t
        k_t = lax.broadcasted_iota(jnp.int32, (rows, PAGE_SIZE), 1)
        process(new_ref[0], k_t <= q_t)
        o_lat = (acc_ref[...] / l_ref[...]).astype(BF16)
        full = jnp.where(head_diag((rows, MLA_HEADS * MLA_V)), _dot(o_lat, wuv_ref[...]), 0.0)
        out = full[0:t]
        for h in range(1, MLA_HEADS):
            out = out + full[h * t:(h + 1) * t]
        o_ref[0] = out


def _mla_paged(q, gk, wuk_t, wuv, new_t, cache_t, pt_flat, n_pages, n_per_step):
    b, t, qd = q.shape
    rows = MLA_HEADS * t
    lat_w = KV_LORA + MLA_ROPE
    grid_spec = pltpu.PrefetchScalarGridSpec(
        num_scalar_prefetch=1,
        grid=(b, n_pages // n_per_step),
        in_specs=[
            pl.BlockSpec((1, t, qd), lambda i, c, pt: (i, 0, 0)),
            pl.BlockSpec(gk.shape, lambda i, c, pt: (0, 0)),
            pl.BlockSpec(wuk_t.shape, lambda i, c, pt: (0, 0)),
            pl.BlockSpec(wuv.shape, lambda i, c, pt: (0, 0)),
            pl.BlockSpec((1, lat_w, PAGE_SIZE), lambda i, c, pt: (i, 0, 0)),
        ] + _page_specs(n_per_step, n_pages, lat_w, PAGE_SIZE),
        out_specs=pl.BlockSpec((1, t, MLA_HEADS * MLA_V), lambda i, c, pt: (i, 0, 0)),
        scratch_shapes=[
            pltpu.VMEM((lat_w, n_per_step * PAGE_SIZE), F32),
            pltpu.VMEM((rows, MLA_HEADS * MLA_NOPE), BF16),
            pltpu.VMEM((rows, MLA_ROPE), BF16),
            pltpu.VMEM((rows, 1), F32), pltpu.VMEM((rows, 1), F32), pltpu.VMEM((rows, KV_LORA), F32),
        ],
    )
    return pl.pallas_call(
        functools.partial(_mla_paged_kernel, n_per_step=n_per_step),
        grid_spec=grid_spec,
        out_shape=jax.ShapeDtypeStruct((b, t, MLA_HEADS * MLA_V), F32),
        compiler_params=_params(("parallel", "arbitrary")),
        name="mla_paged",
    )(pt_flat, q, gk, wuk_t, wuv, new_t, *([cache_t] * n_per_step))


def _even_out_kernel(y_ref, oc_ref, os_ref, ow_ref, ob_ref, misc_ref, w_ref, o_ref):
    gate = misc_ref[...]
    tm = gate.shape[0]
    lane = lax.broadcasted_iota(jnp.int32, (tm, LANE), 1)

    def gate_pair(i, j):
        a = GATE_LANE0 + 3 * (2 * i) + j
        b = GATE_LANE0 + 3 * (2 * i + 1) + j
        return jnp.where(lane < HEAD_DIM, jnp.broadcast_to(gate[:, a:a + 1], (tm, LANE)),
                         jnp.broadcast_to(gate[:, b:b + 1], (tm, LANE)))

    n_a = oc_ref.shape[1] // LANE
    acc = y_ref[...]
    for i in range(n_a):
        sl = slice(i * LANE, (i + 1) * LANE)
        oa = gate_pair(i, 0) * oc_ref[:, sl] + gate_pair(i, 1) * os_ref[:, sl] + gate_pair(i, 2) * ow_ref[:, sl]
        acc = acc + _dot(oa.astype(BF16), w_ref[sl, :])
    acc = acc + _dot(ob_ref[...].astype(BF16), w_ref[n_a * LANE:, :])
    o_ref[...] = acc


def _even_out(y, oc, os_, ow, ob, misc, w, tm):
    m, d = y.shape
    row = lambda i: (i, 0)
    return pl.pallas_call(
        _even_out_kernel,
        grid=(m // tm,),
        in_specs=[pl.BlockSpec((tm, d), row)] + [pl.BlockSpec((tm, 512), row)] * 4
                 + [pl.BlockSpec((tm, LANE), row), pl.BlockSpec(w.shape, lambda i: (0, 0))],
        out_specs=pl.BlockSpec((tm, d), row),
        out_shape=jax.ShapeDtypeStruct((m, d), F32),
        compiler_params=_params(("parallel",)),
        name="even_out",
    )(y, oc, os_, ow, ob, misc, w)


def _odd_out_kernel(y_ref, o_ref_in, w_ref, o_ref):
    o_ref[...] = y_ref[...] + _dot(o_ref_in[...].astype(BF16), w_ref[...])


def _odd_out(y, o, w, tm):
    m, d = y.shape
    row = lambda i: (i, 0)
    return pl.pallas_call(
        _odd_out_kernel,
        grid=(m // tm,),
        in_specs=[pl.BlockSpec((tm, d), row), pl.BlockSpec((tm, o.shape[1]), row),
                  pl.BlockSpec(w.shape, lambda i: (0, 0))],
        out_specs=pl.BlockSpec((tm, d), row),
        out_shape=jax.ShapeDtypeStruct((m, d), F32),
        compiler_params=_params(("parallel",)),
        name="odd_out",
    )(y, o, w)


def _row_tile(m, cap=1024):
    t = cap
    while m % t:
        t //= 2
    return t


def _rope_tables(pos, half, seg_starts):
    inv = ROPE_THETA ** (-jnp.arange(half, dtype=F32) / half)
    ang = pos.astype(F32)[:, None] * inv[None, :]
    cos, sin = jnp.cos(ang), jnp.sin(ang)
    n = pos.shape[0]
    c = jnp.ones((n, LANE), F32)
    s1 = jnp.zeros((n, LANE), F32)
    s2 = jnp.zeros((n, LANE), F32)
    for st in seg_starts:
        c = c.at[:, st:st + half].set(cos).at[:, st + half:st + 2 * half].set(cos)
        s1 = s1.at[:, st:st + half].set(-sin)
        s2 = s2.at[:, st + half:st + 2 * half].set(sin)
    return c, s1, s2


def _tile_tables(tabs, reps):
    return tuple(jnp.tile(a, (reps, 1)) for a in tabs)


def _block_mean_matrix(group):
    i = np.arange(LANE)
    return jnp.asarray((i[:, None] // group == i[None, :] // group) / group, BF16)


def _even_weights(w_in, nsa_g, dsa_g):
    d = w_in.shape[0]
    n_q = NSA_KV_HEADS * NSA_GROUP * HEAD_DIM
    n_kv = 2 * NSA_KV_HEADS * HEAD_DIM
    splits = np.cumsum([n_q, 3 * n_kv, 3 * NSA_KV_HEADS * NSA_GROUP, n_q, n_kv, IDX_HEADS * IDX_DIM, IDX_HEADS])
    q_a, kv_a, gate, q_b, kv_b, qi, wi, ki = jnp.split(w_in, splits.tolist(), axis=1)
    pad = jnp.zeros((d, LANE - ki.shape[1] - gate.shape[1] - wi.shape[1]), w_in.dtype)
    w = jnp.concatenate([q_a, kv_a, q_b, kv_b, qi, ki, gate, wi, pad], axis=1).astype(BF16)
    one = jnp.ones((LANE,), F32)
    pair = lambda g: jnp.tile(g, 2)
    gain = jnp.concatenate(
        [pair(nsa_g[0])] * 4
        + [pair(nsa_g[1]), one, pair(nsa_g[2]), one, pair(nsa_g[3]), one]
        + [pair(dsa_g[0])] * 4 + [pair(dsa_g[1]), one] + [one] * 3).reshape(1, EVEN_TILES * LANE)
    return w, gain


def _compress_weights(pe, w1, w2):
    pe_t = jnp.concatenate([jnp.tile(pe[k], (1, NSA_KV_HEADS)) for k in range(2)], axis=1)
    hid = w1.shape[2]
    w1r = w1.reshape(2, CMP_BLOCK, HEAD_DIM, hid)
    z1 = jnp.zeros_like(w1r)
    w1bd = jnp.concatenate([jnp.concatenate([w1r, z1], axis=3), jnp.concatenate([z1, w1r], axis=3)], axis=2)
    z2 = jnp.zeros_like(w2)
    w2bd = jnp.concatenate([jnp.concatenate([w2, z2], axis=2), jnp.concatenate([z2, w2], axis=2)], axis=1)
    return pe_t, w1bd.astype(BF16), w2bd.astype(BF16)


def _pair_sum_matrix(nc, ns_pad):
    ratio = SEL_BLOCK // CMP_BLOCK
    return jnp.asarray(np.arange(nc)[:, None] // ratio == np.arange(ns_pad)[None, :], F32)


def _pad_heads(w, width):
    r, h, _ = w.shape
    return jnp.pad(w, ((0, 0), (0, 0), (0, LANE - width))).reshape(r, h * LANE)


def _pad_rows(x, n):
    return jnp.pad(x, ((0, 0), (0, n - x.shape[1]), (0, 0)))


def kernel(x_prompt, x_sample, cache_nsa_cmp_kv, cache_nsa_slc_kv, state_nsa_win_kv, cache_dsa_kv, cache_dsa_idx_k, cache_mla_latent, page_table, norm_g, ffn_w_gu, ffn_w_down, even_w_in, nsa_qk_g, nsa_cmp_pe, nsa_cmp_w1, nsa_cmp_w2, dsa_qk_g, even_w_out, odd_w_in, mla_norm_cq, mla_norm_ckv, mla_w_qb, mla_w_uk, mla_w_uv, mla_qk_g, odd_w_out):
    bp, s, d = x_prompt.shape
    bs, t, _ = x_sample.shape
    n_pages = page_table.shape[1]
    past = n_pages * PAGE_SIZE
    depth = norm_g.shape[0]
    n_pool = cache_nsa_cmp_kv.shape[1]
    assert s % CMP_BLOCK == 0 and s % LANE == 0 and t < CMP_BLOCK and t % SUBLANE == 0
    kv_w = 2 * NSA_KV_HEADS * HEAD_DIM
    lat_w = KV_LORA + MLA_ROPE
    yp = x_prompt.reshape(bp * s, d)
    ys = x_sample.reshape(bs * t, d)
    pt_flat = page_table.reshape(-1)
    tm_p = _row_tile(s)
    tm_s = _row_tile(bs * t, 512)
    tm_proj = _row_tile(s, 512)
    tm_odd = _row_tile(s, 256)
    pos_p = jnp.arange(s)
    pos_s = past + jnp.arange(t)
    tq = min(128, s)
    tq_attn = min(128, s)
    tk = min(1024, s)
    pages_per_step = min(16, n_pages)

    nsa_tabs_p = _rope_tables(pos_p, ROT_DIM // 2, (0, HEAD_DIM))
    nsa_tabs_s = _tile_tables(_rope_tables(pos_s, ROT_DIM // 2, (0, HEAD_DIM)), tm_s // t)
    mla_tabs_p = _rope_tables(pos_p, MLA_ROPE // 2, (MLA_NOPE,))
    mla_tabs_s = _tile_tables(_rope_tables(pos_s, MLA_ROPE // 2, (MLA_NOPE,)), tm_s // t)
    bd64 = _block_mean_matrix(HEAD_DIM)
    ones96 = jnp.full((LANE, LANE), 1.0 / MLA_QK, BF16)

    def ffn(y, layer, i, gi, tm):
        return _ffn_half(y, norm_g[layer, gi], ffn_w_gu[layer, i].astype(BF16), ffn_w_down[layer, i].astype(BF16), tm)

    even_p = [[] for _ in range(5)]
    even_s = [[] for _ in range(5)]
    mla_p, mla_s = [], []
    for layer in range(depth):
        li = layer // 2
        yp = ffn(yp, layer, 0, 0, tm_p)
        ys = ffn(ys, layer, 0, 0, tm_s)
        if layer % 2 == 0:
            w_in, gain = _even_weights(even_w_in[li], nsa_qk_g[li], dsa_qk_g[li])
            pe_t, w1bd, w2bd = _compress_weights(nsa_cmp_pe[li], nsa_cmp_w1[li], nsa_cmp_w2[li])
            w_out = even_w_out[li].astype(BF16)
            g1 = norm_g[layer, 1]

            qa, kcmp, kslc, kwin, qb, kvb, qi, misc, ki = _even_project(
                yp, g1, w_in, gain, nsa_tabs_p, s // tm_proj, bd64, tm_proj)
            r3 = lambda a: a.reshape(bp, s, a.shape[1])
            nc = s // CMP_BLOCK
            kvc = _compress_dense(kcmp, pe_t, w1bd, w2bd, _row_tile(bp * s, 4096)).reshape(bp, nc, kv_w)
            ns_pad = -(-(s // SEL_BLOCK) // LANE) * LANE
            oc, mask_slc = _nsa_cmp_select(r3(qa), kvc, _pair_sum_matrix(nc, ns_pad), tq, 0, s, s, False)
            attn = functools.partial(_attention, n_grp=NSA_KV_HEADS, n_rep=NSA_GROUP, dk=HEAD_DIM, dv=HEAD_DIM,
                                     scale=HEAD_DIM ** -0.5, tq=tq_attn, tk=tk, k_tile=0, v_tile=1)
            o_slc = attn(r3(qa), r3(kslc), r3(kslc), mask_slc, mode="array")
            o_win = attn(r3(qa), r3(kwin), r3(kwin), None, mode="window")
            ki_t = jnp.swapaxes(ki.reshape(bp, s, IDX_DIM), 1, 2)
            mask_dsa = _dsa_index_dense(r3(qi), r3(misc), ki_t, tq, min(DSA_TOPK, s // 4))
            o_dsa = attn(r3(qb), r3(kvb), r3(kvb), mask_dsa, mode="array")
            f2 = lambda a: a.reshape(bp * s, a.shape[2])
            yp = _even_out(yp, f2(oc), f2(o_slc), f2(o_win), f2(o_dsa), misc, w_out, tm_proj)
            rows6 = lambda a, n: a.reshape(n, -1, 2, NSA_KV_HEADS, HEAD_DIM)
            w_keep = min(WINDOW, s)
            for lst, a in zip(even_p, (rows6(kcmp, bp), rows6(kslc, bp), rows6(kwin, bp)[:, s - w_keep:],
                                       rows6(kvb, bp), ki.reshape(bp, s, IDX_DIM))):
                lst.append(a)

            qa, kcmp, kslc, kwin, qb, kvb, qi, misc, ki = _even_project(
                ys, g1, w_in, gain, nsa_tabs_s, 1, bd64, tm_s)
            r3 = lambda a: a.reshape(bs, t, a.shape[1])
            new_page = lambda a: jnp.swapaxes(_pad_rows(r3(a), PAGE_SIZE), 1, 2)
            cache2 = lambda c: jnp.moveaxis(c[li].reshape(n_pool, PAGE_SIZE, -1), 1, 2)
            nc = n_pages * (PAGE_SIZE // CMP_BLOCK)
            n_keys = past + t
            lk = (n_pages + 1) * PAGE_SIZE
            kvc = _compress_paged(cache2(cache_nsa_cmp_kv), pt_flat, bs, n_pages, pe_t, w1bd, w2bd,
                                  min(32, n_pages)).reshape(bs, nc, kv_w)
            ns_pad = -(-(lk // SEL_BLOCK) // LANE) * LANE
            oc, mask_slc = _nsa_cmp_select(r3(qa), kvc, _pair_sum_matrix(nc, ns_pad), t, past, n_keys, lk, True)
            paged = functools.partial(_paged_attention, pt_flat=pt_flat, n_pages=n_pages,
                                      n_per_step=pages_per_step, n_grp=NSA_KV_HEADS, n_rep=NSA_GROUP)
            o_slc = paged(r3(qa), mask_slc, new_page(kslc), cache2(cache_nsa_slc_kv))
            kv_win = jnp.concatenate([state_nsa_win_kv[li].reshape(bs, -1, kv_w), r3(kwin)], axis=1)
            wb = kv_win.shape[1] - t
            lw = -(-(wb + t) // LANE) * LANE
            dpos = (past + np.arange(t))[:, None] - (past - wb + np.arange(lw))[None, :]
            ok_w = (dpos >= 0) & (dpos <= WINDOW) & (np.arange(lw)[None, :] < wb + t) & ((past - wb + np.arange(lw))[None, :] >= 0)
            mask_win = jnp.asarray(np.where(ok_w, 0.0, MASK_BIAS)[None, None], BF16)
            kv_win_pad = _pad_rows(kv_win, lw)
            o_win = _attention(r3(qa), kv_win_pad, kv_win_pad, mask_win, n_grp=NSA_KV_HEADS, n_rep=NSA_GROUP,
                               dk=HEAD_DIM, dv=HEAD_DIM, scale=HEAD_DIM ** -0.5, mode="array", tq=t, tk=lw,
                               k_tile=0, v_tile=1, skip=False)
            mask_dsa = _dsa_index_paged(r3(qi), r3(misc), new_page(ki), cache2(cache_dsa_idx_k), pt_flat,
                                        n_pages, min(32, n_pages), min(DSA_TOPK, n_keys // 4))
            o_dsa = paged(r3(qb), mask_dsa, new_page(kvb), cache2(cache_dsa_kv))
            f2 = lambda a: a.reshape(bs * t, a.shape[2])
            ys = _even_out(ys, f2(oc), f2(o_slc), f2(o_win), f2(o_dsa), misc, w_out, tm_s)
            w_keep = min(WINDOW, wb + t)
            for lst, a in zip(even_s, (rows6(kcmp, bs), rows6(kslc, bs), rows6(kv_win[:, wb + t - w_keep:], bs),
                                       rows6(kvb, bs), ki.reshape(bs, t, IDX_DIM))):
                lst.append(a)
        else:
            cq, ckv, kr = jnp.split(odd_w_in[li], [Q_LORA, Q_LORA + KV_LORA], axis=1)
            z = lambda n: jnp.zeros((d, n), odd_w_in.dtype)
            w_in = jnp.concatenate([cq, ckv, z(MLA_NOPE), kr, z(LANE - MLA_QK)], axis=1).astype(BF16)
            wqb = _pad_heads(mla_w_qb[li], MLA_QK).astype(BF16)
            wuk_pad = _pad_heads(mla_w_uk[li], MLA_NOPE).astype(BF16)
            wuk_t = mla_w_uk[li].reshape(KV_LORA, MLA_HEADS * MLA_NOPE).T.astype(BF16)
            wuv = mla_w_uv[li].reshape(KV_LORA, MLA_HEADS * MLA_V).astype(BF16)
            pad_gain = lambda g: jnp.tile(jnp.pad(g, (0, LANE - MLA_QK)), MLA_HEADS).reshape(1, MLA_HEADS * LANE)
            gq, gk = pad_gain(mla_qk_g[li, 0]), pad_gain(mla_qk_g[li, 1])
            ncq = mla_norm_cq[li].reshape(1, Q_LORA)
            nckv = mla_norm_ckv[li].reshape(1, KV_LORA)
            w_out = odd_w_out[li].astype(BF16)
            g1 = norm_g[layer, 1]

            q, lat, k, v = _odd_project(yp, g1, w_in, ncq, nckv, wqb, gq, mla_tabs_p, s // tm_odd, ones96,
                                        (wuk_pad, gk, wuv), tm_odd)
            r3 = lambda a: a.reshape(bp, s, a.shape[1])
            o = _attention(r3(q), r3(k), r3(v), None, n_grp=MLA_HEADS, n_rep=1, dk=MLA_PAD, dv=MLA_V,
                           scale=MLA_QK ** -0.5, mode="causal", tq=min(256, s), tk=tk)
            yp = _odd_out(yp, o.reshape(bp * s, -1), w_out, tm_p)
            mla_p.append(lat.reshape(bp, s, lat_w))

            q, lat = _odd_project(ys, g1, w_in, ncq, nckv, wqb, gq, mla_tabs_s, 1, ones96, None, tm_s)
            lat3 = lat.reshape(bs, t, lat_w)
            o = _mla_paged(q.reshape(bs, t, -1), gk, wuk_t, wuv, jnp.swapaxes(_pad_rows(lat3, PAGE_SIZE), 1, 2),
                           jnp.swapaxes(cache_mla_latent[li], 1, 2), pt_flat, n_pages, pages_per_step)
            ys = _odd_out(ys, o.reshape(bs * t, -1), w_out, tm_s)
            mla_s.append(lat3)
        yp = ffn(yp, layer, 1, 2, tm_p)
        ys = ffn(ys, layer, 1, 2, tm_s)

    stack = lambda lst: jnp.stack(lst)
    cmp_p, slc_p, win_p, dsa_p, idx_p = [stack(a) for a in even_p]
    cmp_s, slc_s, win_s, dsa_s, idx_s = [stack(a) for a in even_s]
    return (yp.reshape(bp, s, d), ys.reshape(bs, t, d), cmp_p, cmp_s, slc_p, slc_s, win_p, win_s,
            dsa_p, dsa_s, idx_p, idx_s, stack(mla_p), stack(mla_s))
```

```python
import functools
import math

import numpy as np
import jax
import jax.numpy as jnp
from jax import lax
from jax.experimental import pallas as pl
from jax.experimental.pallas import tpu as pltpu

F32 = jnp.float32
BF16 = jnp.bfloat16

LANE = 128
SUBLANE = 8
MIB = 1024 * 1024

PAGE_SIZE = 128
HEAD_DIM = 64
ROT_DIM = HEAD_DIM // 4
ROPE_THETA = 500000.0
NORM_EPS = 1e-6
NEG_BIG = -1e30
MASK_BIAS = -2e30
NEG_INF_KEY = int(np.int32(np.array(-np.inf, np.float32).view(np.int32)) ^ np.int32(0x7FFFFFFF))
NSA_KV_HEADS = 2
NSA_GROUP = 4
CMP_BLOCK = 32
CMP_PITCH = CMP_BLOCK + SUBLANE
SEL_BLOCK = 64
N_SEL = 16
WINDOW = 512
FORCE_BONUS = float(NSA_GROUP + 1)
DSA_TOPK = 256
IDX_HEADS = 4
IDX_DIM = 64
MLA_HEADS = 16
MLA_NOPE = 64
MLA_ROPE = 32
MLA_V = 64
MLA_QK = MLA_NOPE + MLA_ROPE
MLA_PAD = LANE
Q_LORA = 256
KV_LORA = 256
GATE_LANE0 = IDX_DIM
WI_LANE0 = GATE_LANE0 + 3 * NSA_KV_HEADS * NSA_GROUP
HI = lax.Precision.HIGHEST


def _params(sem, vmem_mib=48):
    return pltpu.CompilerParams(dimension_semantics=sem, vmem_limit_bytes=vmem_mib * MIB)


def _dot(a, b, precision=None):
    return jnp.dot(a, b, preferred_element_type=F32, precision=precision)


def _dot_nt(a, b):
    return lax.dot_general(a, b, (((1,), (1,)), ((), ())), preferred_element_type=F32)


def _rms_rows(x, g):
    ms = jnp.mean(x * x, axis=-1, keepdims=True)
    return x * lax.rsqrt(ms + NORM_EPS) * g


def _group_mean_sq(x, ones_bf16):
    sq = x * x
    hi = sq.astype(BF16)
    lo = (sq - hi.astype(F32)).astype(BF16)
    return _dot(hi, ones_bf16) + _dot(lo, ones_bf16)


def _rope_lanes(x, c, s1, s2, half):
    return x * c + pltpu.roll(x, LANE - half, 1) * s1 + pltpu.roll(x, half, 1) * s2


def _ffn_kernel(x_ref, g_ref, wa_ref, wu_ref, wd_ref, o_ref, xn_ref, acc_ref):
    f = pl.program_id(1)

    @pl.when(f == 0)
    def _():
        xn_ref[...] = _rms_rows(x_ref[...], g_ref[...]).astype(BF16)
        acc_ref[...] = jnp.zeros_like(acc_ref)

    xn = xn_ref[...]
    a = _dot(xn, wa_ref[...])
    u = _dot(xn, wu_ref[...])
    h = (a * jax.nn.sigmoid(a) * u).astype(BF16)
    acc_ref[...] += _dot(h, wd_ref[...])

    @pl.when(f == pl.num_programs(1) - 1)
    def _():
        o_ref[...] = x_ref[...] + 0.5 * acc_ref[...]


def _ffn_half(x, g, w_gu, w_down, tm, tf=256):
    m, d = x.shape
    d_ff = w_down.shape[0]
    n_f = d_ff // tf
    return pl.pallas_call(
        _ffn_kernel,
        grid=(m // tm, n_f),
        in_specs=[
            pl.BlockSpec((tm, d), lambda i, f: (i, 0)),
            pl.BlockSpec((1, d), lambda i, f: (0, 0)),
            pl.BlockSpec((d, tf), lambda i, f: (0, f)),
            pl.BlockSpec((d, tf), lambda i, f: (0, n_f + f)),
            pl.BlockSpec((tf, d), lambda i, f: (f, 0)),
        ],
        out_specs=pl.BlockSpec((tm, d), lambda i, f: (i, 0)),
        out_shape=jax.ShapeDtypeStruct((m, d), F32),
        scratch_shapes=[pltpu.VMEM((tm, d), BF16), pltpu.VMEM((tm, d), F32)],
        compiler_params=_params(("parallel", "arbitrary")),
        name="ffn_half",
    )(x, g.reshape(1, d), w_gu, w_gu, w_down)


EVEN_TILES = 19


def _even_proj_kernel(x_ref, g_ref, w_ref, gain_ref, c_ref, s1_ref, s2_ref, bd_ref,
                      qa_ref, cmp_ref, slc_ref, win_ref, qb_ref, kvb_ref, qi_ref, misc_ref, ki_ref):
    xn = _rms_rows(x_ref[...], g_ref[...]).astype(BF16)
    proj = _dot(xn, w_ref[...])
    c, s1, s2 = c_ref[...], s1_ref[...], s2_ref[...]
    bd = bd_ref[...]
    half = ROT_DIM // 2

    def tile(i):
        return proj[:, i * LANE:(i + 1) * LANE]

    def normed(i):
        x = tile(i)
        return x * lax.rsqrt(_group_mean_sq(x, bd) + NORM_EPS) * gain_ref[:, i * LANE:(i + 1) * LANE]

    def rope(x):
        return _rope_lanes(x, c, s1, s2, half)

    for i in range(4):
        qa_ref[:, i * LANE:(i + 1) * LANE] = rope(normed(i))
    for br, ref in enumerate((cmp_ref, slc_ref, win_ref)):
        ref[:, :LANE] = rope(normed(4 + 2 * br))
        ref[:, LANE:] = tile(5 + 2 * br)
    for i in range(4):
        qb_ref[:, i * LANE:(i + 1) * LANE] = rope(normed(10 + i))
    kvb_ref[:, :LANE] = rope(normed(14))
    kvb_ref[:, LANE:] = tile(15)
    for i in range(2):
        qi_ref[:, i * LANE:(i + 1) * LANE] = rope(tile(16 + i))
    m = tile(18)
    mr = rope(m)
    lane = lax.broadcasted_iota(jnp.int32, m.shape, 1)
    misc = jnp.where(lane < GATE_LANE0, mr,
                     jnp.where(lane < WI_LANE0, jax.nn.sigmoid(m),
                               jnp.where(lane < WI_LANE0 + IDX_HEADS, m * IDX_HEADS ** -0.5, 0.0)))
    misc_ref[...] = misc
    ki_ref[...] = mr[:, :IDX_DIM]


def _even_project(x, g, w, gain, tabs, tab_period, bd, tm):
    m, d = x.shape
    c, s1, s2 = tabs
    row = lambda i: (i, 0)
    fixed = lambda i: (0, 0)
    tab = lambda i: (i % tab_period, 0)
    widths = (512, 256, 256, 256, 512, 256, 256, LANE, IDX_DIM)
    return pl.pallas_call(
        _even_proj_kernel,
        grid=(m // tm,),
        in_specs=[
            pl.BlockSpec((tm, d), row),
            pl.BlockSpec((1, d), fixed),
            pl.BlockSpec(w.shape, fixed),
            pl.BlockSpec(gain.shape, fixed),
            pl.BlockSpec((tm, LANE), tab), pl.BlockSpec((tm, LANE), tab), pl.BlockSpec((tm, LANE), tab),
            pl.BlockSpec(bd.shape, fixed),
        ],
        out_specs=[pl.BlockSpec((tm, wd), row) for wd in widths],
        out_shape=[jax.ShapeDtypeStruct((m, wd), F32) for wd in widths],
        compiler_params=_params(("parallel",)),
        name="even_project",
    )(x, g.reshape(1, d), w, gain, c, s1, s2, bd)


def _compress_rows(row_refs, n_blk, pe_ref, w1_ref, w2_ref):
    outs = []
    for k in range(2):
        acc = jnp.zeros((n_blk, 2 * LANE), F32)
        for j in range(CMP_BLOCK):
            xj = row_refs[k][pl.ds(j, n_blk, stride=CMP_PITCH), :] + pe_ref[j:j + 1, k * LANE:(k + 1) * LANE]
            acc = acc + _dot(xj.astype(BF16), w1_ref[k, j])
        outs.append(_dot(jax.nn.gelu(acc).astype(BF16), w2_ref[k]))
    return jnp.concatenate(outs, axis=-1)


def _compress_kernel(k_ref, v_ref, pe_ref, w1_ref, w2_ref, o_ref, kbuf_ref, vbuf_ref):
    n_blk = o_ref.shape[0]
    for n in range(n_blk):
        kbuf_ref[n * CMP_PITCH:n * CMP_PITCH + CMP_BLOCK, :] = k_ref[n * CMP_BLOCK:(n + 1) * CMP_BLOCK, :]
        vbuf_ref[n * CMP_PITCH:n * CMP_PITCH + CMP_BLOCK, :] = v_ref[n * CMP_BLOCK:(n + 1) * CMP_BLOCK, :]
    o_ref[...] = _compress_rows((kbuf_ref, vbuf_ref), n_blk, pe_ref, w1_ref, w2_ref)


def _compress_dense(rows, pe, w1, w2, t_rows):
    m = rows.shape[0]
    n_blk = t_rows // CMP_BLOCK
    return pl.pallas_call(
        _compress_kernel,
        grid=(m // t_rows,),
        in_specs=[
            pl.BlockSpec((t_rows, LANE), lambda i: (i, 0)),
            pl.BlockSpec((t_rows, LANE), lambda i: (i, 1)),
            pl.BlockSpec(pe.shape, lambda i: (0, 0)),
            pl.BlockSpec(w1.shape, lambda i: (0, 0, 0, 0)),
            pl.BlockSpec(w2.shape, lambda i: (0, 0, 0)),
        ],
        out_specs=pl.BlockSpec((n_blk, 2 * LANE), lambda i: (i, 0)),
        out_shape=jax.ShapeDtypeStruct((m // CMP_BLOCK, 2 * LANE), F32),
        scratch_shapes=[pltpu.VMEM((n_blk * CMP_PITCH, LANE), F32)] * 2,
        compiler_params=_params(("parallel",)),
        name="nsa_compress_dense",
    )(rows, rows, pe, w1, w2)


def _page_specs(n_per_step, n_pages, rows, cols):
    return [pl.BlockSpec((None, rows, cols),
                         lambda b, c, pt, p=p: (pt[b * n_pages + c * n_per_step + p], 0, 0))
            for p in range(n_per_step)]


def _compress_paged_kernel(pt_ref, pe_ref, w1_ref, w2_ref, *rest, n_per_step):
    pages = rest[:n_per_step]
    o_ref, kbuf_ref, vbuf_ref = rest[n_per_step:]
    blk_per_page = PAGE_SIZE // CMP_BLOCK
    for p in range(n_per_step):
        k_rows, v_rows = pages[p][:LANE, :].T, pages[p][LANE:, :].T
        for i in range(blk_per_page):
            r0 = (p * blk_per_page + i) * CMP_PITCH
            kbuf_ref[r0:r0 + CMP_BLOCK, :] = k_rows[i * CMP_BLOCK:(i + 1) * CMP_BLOCK]
            vbuf_ref[r0:r0 + CMP_BLOCK, :] = v_rows[i * CMP_BLOCK:(i + 1) * CMP_BLOCK]
    o_ref[...] = _compress_rows((kbuf_ref, vbuf_ref), o_ref.shape[0], pe_ref, w1_ref, w2_ref)


def _compress_paged(cache, pt_flat, n_seq, n_pages, pe, w1, w2, n_per_step):
    blk_per_step = n_per_step * PAGE_SIZE // CMP_BLOCK
    n_ch = n_pages // n_per_step
    grid_spec = pltpu.PrefetchScalarGridSpec(
        num_scalar_prefetch=1,
        grid=(n_seq, n_ch),
        in_specs=[
            pl.BlockSpec(pe.shape, lambda b, c, pt: (0, 0)),
            pl.BlockSpec(w1.shape, lambda b, c, pt: (0, 0, 0, 0)),
            pl.BlockSpec(w2.shape, lambda b, c, pt: (0, 0, 0)),
        ] + _page_specs(n_per_step, n_pages, 2 * LANE, PAGE_SIZE),
        out_specs=pl.BlockSpec((blk_per_step, 2 * LANE), lambda b, c, pt: (b * n_ch + c, 0)),
        scratch_shapes=[pltpu.VMEM((blk_per_step * CMP_PITCH, LANE), F32)] * 2,
    )
    return pl.pallas_call(
        functools.partial(_compress_paged_kernel, n_per_step=n_per_step),
        grid_spec=grid_spec,
        out_shape=jax.ShapeDtypeStruct((n_seq * n_pages * PAGE_SIZE // CMP_BLOCK, 2 * LANE), F32),
        compiler_params=_params(("parallel", "arbitrary")),
        name="nsa_compress_paged",
    )(pt_flat, pe, w1, w2, *([cache] * n_per_step))


def _topk_mask(score, idx, k, idx_bits):
    def count(pred):
        v = jnp.where(pred, 1.0, 0.0)
        if v.ndim == 3:
            v = jnp.sum(v, axis=0)
        return jnp.sum(v, axis=-1, keepdims=True)

    rows = score.shape[-2]
    bits = pltpu.bitcast(score + 0.0, jnp.int32)
    key = jnp.where(bits < 0, bits ^ jnp.int32(0x7FFFFFFF), bits)
    kf = float(k)
    int_min = jnp.full((rows, 1), -2 ** 31, jnp.int32)
    t0 = jnp.where(count(key >= 0) >= kf, jnp.zeros((rows, 1), jnp.int32), int_min)

    def value_bit(i, carry):
        t, n_ge = carry
        cand = t | jnp.left_shift(jnp.int32(1), 30 - i)
        n = count(key >= cand)
        keep = n >= kf
        return jnp.where(keep, cand, t), jnp.where(keep, n, n_ge)

    thr, n_ge = lax.fori_loop(0, 31, value_bit, (t0, count(key >= t0)))
    tie_idx = jnp.where(key == thr, idx, jnp.int32(2 ** 30))

    def tie_cut():
        need = kf - count(key > thr)

        def index_bit(i, j):
            cand = j | jnp.left_shift(jnp.int32(1), idx_bits - 1 - i)
            return jnp.where(count(tie_idx < cand) <= need, cand, j)

        return lax.fori_loop(0, idx_bits, index_bit, jnp.zeros((rows, 1), jnp.int32))

    j_cut = lax.cond(jnp.max(n_ge) > kf, tie_cut, lambda: jnp.full((rows, 1), 2 ** 30, jnp.int32))
    return jnp.where(key > thr, 1.0, jnp.where(tie_idx < j_cut, 1.0, 0.0))


def _nsa_cmp_kernel(q_ref, kvc_ref, pm_ref, oc_ref, mask_ref, *, tq, pos0, n_keys, lk, chunk_major):
    nc = kvc_ref.shape[1]
    nsp = pm_ref.shape[1]
    ns = -(-n_keys // SEL_BLOCK)
    n_sel = min(N_SEL, ns)
    r_rows = NSA_GROUP * tq
    q = q_ref[0]
    kvc = kvc_ref[0]
    pos_t = pos0 + pl.program_id(1) * tq + lax.broadcasted_iota(jnp.int32, (tq, 1), 0)
    pos_r = jnp.concatenate([pos_t] * NSA_GROUP, axis=0)
    c_end = (lax.broadcasted_iota(jnp.int32, (r_rows, nc), 1) + 1) * CMP_BLOCK - 1
    vis = c_end <= pos_r
    blk = lax.broadcasted_iota(jnp.int32, (tq, nsp), 1)
    cur = pos_t // SEL_BLOCK
    forced = jnp.where(blk == 0, FORCE_BONUS, jnp.where(blk == cur, FORCE_BONUS, jnp.where(blk == cur - 1, FORCE_BONUS, 0.0)))
    lane = lax.broadcasted_iota(jnp.int32, (tq, LANE), 1)
    scores = []
    for g in range(NSA_KV_HEADS):
        qg = jnp.concatenate([q[:, (g * NSA_GROUP + r) * HEAD_DIM:(g * NSA_GROUP + r + 1) * HEAD_DIM]
                              for r in range(NSA_GROUP)], axis=0).astype(BF16)
        kg = kvc[:, g * HEAD_DIM:(g + 1) * HEAD_DIM].astype(BF16)
        vg = kvc[:, LANE + g * HEAD_DIM:LANE + (g + 1) * HEAD_DIM].astype(BF16)
        s = jnp.where(vis, _dot_nt(qg, kg) * HEAD_DIM ** -0.5, -jnp.inf)
        mx = jnp.max(s, axis=-1, keepdims=True)
        mx = jnp.where(mx > -jnp.inf, mx, 0.0)
        e = jnp.where(vis, jnp.exp(s - mx), 0.0)
        p = e / jnp.maximum(jnp.sum(e, axis=-1, keepdims=True), 1e-30)
        o = _dot(p.astype(BF16), vg)
        for r in range(NSA_GROUP):
            h = g * NSA_GROUP + r
            oc_ref[0, :, h * HEAD_DIM:(h + 1) * HEAD_DIM] = o[r * tq:(r + 1) * tq]
        imp = p[0:tq]
        for r in range(1, NSA_GROUP):
            imp = imp + p[r * tq:(r + 1) * tq]
        imp = _dot(imp, pm_ref[...], precision=HI)
        score = jnp.where(blk * SEL_BLOCK <= pos_t, imp + forced, -1.0)
        scores.append(jnp.where(blk < ns, score, -jnp.inf))
    score = jnp.concatenate(scores, axis=0)
    sel_all = _topk_mask(score, jnp.concatenate([blk] * NSA_KV_HEADS, axis=0), n_sel, nsp.bit_length())
    sel_all = jnp.where(score >= 0.0, sel_all, 0.0)
    for g in range(NSA_KV_HEADS):
        sel = sel_all[g * tq:(g + 1) * tq]
        for c in range(lk // LANE):
            lo = jnp.broadcast_to(sel[:, 2 * c:2 * c + 1], (tq, LANE))
            hi = jnp.broadcast_to(sel[:, 2 * c + 1:2 * c + 2], (tq, LANE))
            chunk = jnp.where(lane < SEL_BLOCK, lo, hi)
            chunk = jnp.where(c * LANE + lane <= pos_t, chunk, 0.0)
            bias = jnp.where(chunk > 0.5, 0.0, MASK_BIAS).astype(mask_ref.dtype)
            if chunk_major:
                mask_ref[0, g, c] = bias
            else:
                mask_ref[0, g, :, c * LANE:(c + 1) * LANE] = bias


def _nsa_cmp_select(q, kvc, pm, tq, pos0, n_keys, lk, chunk_major):
    b, sq, _ = q.shape
    nc = kvc.shape[1]
    if chunk_major:
        mask_spec = pl.BlockSpec((1, NSA_KV_HEADS, lk // LANE, tq, LANE), lambda i, j: (i, 0, 0, j, 0))
        mask_shape = jax.ShapeDtypeStruct((b, NSA_KV_HEADS, lk // LANE, sq, LANE), F32)
    else:
        mask_spec = pl.BlockSpec((1, NSA_KV_HEADS, tq, lk), lambda i, j: (i, 0, j, 0))
        mask_shape = jax.ShapeDtypeStruct((b, NSA_KV_HEADS, sq, lk), BF16)
    return pl.pallas_call(
        functools.partial(_nsa_cmp_kernel, tq=tq, pos0=pos0, n_keys=n_keys, lk=lk, chunk_major=chunk_major),
        grid=(b, sq // tq),
        in_specs=[
            pl.BlockSpec((1, tq, 512), lambda i, j: (i, j, 0)),
            pl.BlockSpec((1, nc, 2 * LANE), lambda i, j: (i, 0, 0)),
            pl.BlockSpec(pm.shape, lambda i, j: (0, 0)),
        ],
        out_specs=[pl.BlockSpec((1, tq, 512), lambda i, j: (i, j, 0)), mask_spec],
        out_shape=[jax.ShapeDtypeStruct((b, sq, 512), F32), mask_shape],
        compiler_params=_params(("parallel", "parallel")),
        name="nsa_cmp_select",
    )(q, kvc, pm)


def _idx_scores(qi, wi, ki_t):
    out = None
    for h in range(IDX_HEADS):
        lg = _dot(qi[:, h * IDX_DIM:(h + 1) * IDX_DIM].astype(BF16), ki_t) * IDX_DIM ** -0.5
        term = wi[:, WI_LANE0 + h:WI_LANE0 + h + 1] * jnp.maximum(lg, 0.0)
        out = term if out is None else out + term
    return out


def _sort_key(score):
    bits = pltpu.bitcast(score + 0.0, jnp.int32)
    return jnp.where(bits < 0, bits ^ jnp.int32(0x7FFFFFFF), bits)


def _topk_bias_chunks(key_ref, n_c, k, idx_bits, write_chunk):
    _, rows, w = key_ref.shape
    kf = float(k)
    neg_inf_key = jnp.int32(NEG_INF_KEY)

    def lane_sum(v):
        out = v[:, 0:LANE]
        for i in range(1, w // LANE):
            out = out + v[:, i * LANE:(i + 1) * LANE]
        return out

    def count(pred_of_chunk):
        def body(c, acc):
            return acc + lane_sum(jnp.where(pred_of_chunk(c, key_ref[c]), 1.0, 0.0))
        acc = lax.fori_loop(0, n_c, body, jnp.zeros((rows, LANE), F32))
        return jnp.sum(acc, axis=-1, keepdims=True)

    def count_ge(t):
        tb = jnp.broadcast_to(t, (rows, w))
        return count(lambda c, key: key >= tb)

    int_min = jnp.full((rows, 1), -2 ** 31, jnp.int32)
    t0 = jnp.where(count_ge(jnp.zeros((rows, 1), jnp.int32)) >= kf, jnp.zeros((rows, 1), jnp.int32), int_min)

    def value_bit(i, carry):
        t, n_ge = carry
        cand = t | jnp.left_shift(jnp.int32(1), 30 - i)
        n = count_ge(cand)
        keep = n >= kf
        return jnp.where(keep, cand, t), jnp.where(keep, n, n_ge)

    thr, n_ge = lax.fori_loop(0, 31, value_bit, (t0, count_ge(t0)))
    lane = lax.broadcasted_iota(jnp.int32, (rows, w), 1)

    def tie_cut():
        thr_b = jnp.broadcast_to(thr, (rows, w))
        need = kf - count(lambda c, key: key > thr_b)

        def index_bit(i, j):
            cand = j | jnp.left_shift(jnp.int32(1), idx_bits - 1 - i)
            cb = jnp.broadcast_to(cand, (rows, w))
            n = count(lambda c, key: jnp.where(key == thr_b, c * w + lane, jnp.int32(2 ** 30)) < cb)
            return jnp.where(n <= need, cand, j)

        return lax.fori_loop(0, idx_bits, index_bit, jnp.zeros((rows, 1), jnp.int32))

    j_cut = lax.cond(jnp.max(n_ge) > kf, tie_cut, lambda: jnp.full((rows, 1), 2 ** 30, jnp.int32))
    thr_b = jnp.broadcast_to(thr, (rows, w))
    cut_b = jnp.broadcast_to(j_cut, (rows, w))

    def emit(c, _):
        key = key_ref[c]
        tie_ok = jnp.where(key == thr_b, c * w + lane, jnp.int32(2 ** 30)) < cut_b
        sel = jnp.where(key > thr_b, 0.0, jnp.where(tie_ok, 0.0, MASK_BIAS))
        write_chunk(c, jnp.where(key > neg_inf_key, sel, MASK_BIAS))
        return 0

    lax.fori_loop(0, n_c, emit, 0)


def _dsa_index_kernel(qi_ref, mq_ref, kit_ref, bias_ref, key_ref, *, tq, w, k_top):
    sk = kit_ref.shape[2]
    n_w = sk // w
    q_lo = pl.program_id(1) * tq
    n_c = (q_lo + tq + w - 1) // w
    qi, mq = qi_ref[0], mq_ref[0]
    pos_q = q_lo + lax.broadcasted_iota(jnp.int32, (tq, 1), 0)
    lane = lax.broadcasted_iota(jnp.int32, (tq, w), 1)
    for c in range(n_w):
        @pl.when(c < n_c)
        def _(c=c):
            score = _idx_scores(qi, mq, kit_ref[0, :, c * w:(c + 1) * w].astype(BF16))
            key_ref[c] = _sort_key(jnp.where(c * w + lane <= pos_q, score, -jnp.inf))

        @pl.when(c >= n_c)
        def _(c=c):
            bias_ref[0, 0, :, c * w:(c + 1) * w] = jnp.full((tq, w), MASK_BIAS, BF16)

    def write_chunk(c, bias):
        for cc in range(n_w):
            @pl.when(c == cc)
            def _(cc=cc):
                bias_ref[0, 0, :, cc * w:(cc + 1) * w] = bias.astype(BF16)

    _topk_bias_chunks(key_ref, n_c, k_top, sk.bit_length(), write_chunk)


def _dsa_index_dense(qi, misc, ki_t, tq, k_top, w=512):
    b, s, _ = qi.shape
    w = min(w, s)
    return pl.pallas_call(
        functools.partial(_dsa_index_kernel, tq=tq, w=w, k_top=k_top),
        grid=(b, s // tq),
        in_specs=[
            pl.BlockSpec((1, tq, IDX_HEADS * IDX_DIM), lambda i, j: (i, j, 0)),
            pl.BlockSpec((1, tq, LANE), lambda i, j: (i, j, 0)),
            pl.BlockSpec((1, IDX_DIM, s), lambda i, j: (i, 0, 0)),
        ],
        out_specs=pl.BlockSpec((1, 1, tq, s), lambda i, j: (i, 0, j, 0)),
        out_shape=jax.ShapeDtypeStruct((b, 1, s, s), BF16),
        scratch_shapes=[pltpu.VMEM((s // w, tq, w), jnp.int32)],
        compiler_params=_params(("parallel", "parallel")),
        name="dsa_index_dense",
    )(qi, misc, ki_t)


def _dsa_index_paged_kernel(pt_ref, qi_ref, mq_ref, new_ref, *rest, n_per_step, n_pages, t_new, k_top):
    pages = rest[:n_per_step]
    bias_ref, key_ref = rest[n_per_step], rest[n_per_step + 1]
    c = pl.program_id(1)
    n_tiles = n_pages + 1
    qi, mq = qi_ref[0], mq_ref[0]
    for p in range(n_per_step):
        key_ref[c * n_per_step + p] = _idx_scores(qi, mq, pages[p][...].astype(BF16))

    @pl.when(c == pl.num_programs(1) - 1)
    def _():
        s_new = _idx_scores(qi, mq, new_ref[0].astype(BF16))
        row = lax.broadcasted_iota(jnp.int32, (t_new, LANE), 0)
        col = lax.broadcasted_iota(jnp.int32, (t_new, LANE), 1)
        key_ref[n_pages] = jnp.where(col <= row, s_new, -jnp.inf)
        score = key_ref[0:n_tiles]
        idx = (lax.broadcasted_iota(jnp.int32, score.shape, 0) * LANE
               + lax.broadcasted_iota(jnp.int32, score.shape, 2))
        sel = _topk_mask(score, idx, k_top, (n_tiles * LANE).bit_length())
        bias_ref[0, 0] = jnp.where(score > -jnp.inf, jnp.where(sel > 0.5, 0.0, MASK_BIAS), MASK_BIAS)


def _dsa_index_paged(qi, misc, ki_new_t, cache_idx_t, pt_flat, n_pages, n_per_step, k_top):
    b, t, _ = qi.shape
    grid_spec = pltpu.PrefetchScalarGridSpec(
        num_scalar_prefetch=1,
        grid=(b, n_pages // n_per_step),
        in_specs=[
            pl.BlockSpec((1, t, IDX_HEADS * IDX_DIM), lambda i, c, pt: (i, 0, 0)),
            pl.BlockSpec((1, t, LANE), lambda i, c, pt: (i, 0, 0)),
            pl.BlockSpec((1, IDX_DIM, PAGE_SIZE), lambda i, c, pt: (i, 0, 0)),
        ] + _page_specs(n_per_step, n_pages, IDX_DIM, PAGE_SIZE),
        out_specs=pl.BlockSpec((1, 1, n_pages + 1, t, LANE), lambda i, c, pt: (i, 0, 0, 0, 0)),
        scratch_shapes=[pltpu.VMEM((n_pages + 1, t, LANE), F32)],
    )
    return pl.pallas_call(
        functools.partial(_dsa_index_paged_kernel, n_per_step=n_per_step, n_pages=n_pages, t_new=t, k_top=k_top),
        grid_spec=grid_spec,
        out_shape=jax.ShapeDtypeStruct((b, 1, n_pages + 1, t, LANE), F32),
        compiler_params=_params(("parallel", "arbitrary")),
        name="dsa_index_paged",
    )(pt_flat, qi, misc, ki_new_t, *([cache_idx_t] * n_per_step))


def _stack_heads(q, g, n_rep, dk):
    return jnp.concatenate([q[:, (g * n_rep + r) * dk:(g * n_rep + r + 1) * dk] for r in range(n_rep)],
                           axis=0).astype(BF16)


def _online_update(s, g, m_ref, l_ref, acc_ref, pv):
    m_prev = m_ref[g]
    m_new = jnp.maximum(m_prev, jnp.max(s, axis=-1, keepdims=True))
    p = jnp.exp(s - m_new)
    corr = jnp.exp(m_prev - m_new)
    l_ref[g] = l_ref[g] * corr + jnp.sum(p, axis=-1, keepdims=True)
    acc_ref[g] = acc_ref[g] * corr + pv(p.astype(BF16))
    m_ref[g] = m_new


def _attn_init(m_ref, l_ref, acc_ref):
    m_ref[...] = jnp.full(m_ref.shape, NEG_BIG, F32)
    l_ref[...] = jnp.zeros_like(l_ref)
    acc_ref[...] = jnp.zeros_like(acc_ref)


def _attn_finish(o_ref, l_ref, acc_ref, n_grp, n_rep, tq, dv):
    for g in range(n_grp):
        o = acc_ref[g] / jnp.maximum(l_ref[g], 1e-30)
        for r in range(n_rep):
            h = g * n_rep + r
            o_ref[0, :, h * dv:(h + 1) * dv] = o[r * tq:(r + 1) * tq]


def _tile_rows(x, n_rep):
    return x if n_rep == 1 else jnp.concatenate([x] * n_rep, axis=0)


def _attn_kernel(*refs, n_grp, n_rep, dk, dv, scale, mode, tq, tk, window, mask_groups, skip):
    if mode == "array":
        q_ref, k_ref, v_ref, mask_ref, o_ref, m_ref, l_ref, acc_ref = refs
    else:
        q_ref, k_ref, v_ref, o_ref, m_ref, l_ref, acc_ref = refs
        mask_ref = None
    qi, kc = pl.program_id(1), pl.program_id(2)

    @pl.when(kc == 0)
    def _():
        _attn_init(m_ref, l_ref, acc_ref)

    q_lo = qi * tq
    relevant = kc * tk <= q_lo + tq - 1 if skip else kc >= 0
    if mode == "window":
        relevant = jnp.logical_and(relevant, kc * tk + tk - 1 >= q_lo - window)

    @pl.when(relevant)
    def _():
        q, k, v = q_ref[0], k_ref[0], v_ref[0]
        q_scale, s_scale = (scale, None) if math.frexp(scale)[0] == 0.5 else (None, scale)
        if q_scale is not None:
            q = q * q_scale
        if mode != "array":
            pos_q = _tile_rows(q_lo + lax.broadcasted_iota(jnp.int32, (tq, 1), 0), n_rep)
            dpos = pos_q - (kc * tk + lax.broadcasted_iota(jnp.int32, (n_rep * tq, tk), 1))
            ok_pos = dpos >= 0 if mode == "causal" else jnp.abs(2 * dpos - window) <= window
        for g in range(n_grp):
            qg = _stack_heads(q, g, n_rep, dk)
            kg = k[:, g * dk:(g + 1) * dk].astype(BF16)
            vg = v[:, g * dv:(g + 1) * dv].astype(BF16)
            s = _dot_nt(qg, kg)
            if s_scale is not None:
                s = s * s_scale
            if mode == "array":
                s = s + _tile_rows(mask_ref[0, g if mask_groups > 1 else 0].astype(F32), n_rep)
            else:
                s = jnp.where(ok_pos, s, MASK_BIAS)
            _online_update(s, g, m_ref, l_ref, acc_ref, lambda pb, vg=vg: _dot(pb, vg))

    @pl.when(kc == pl.num_programs(2) - 1)
    def _():
        _attn_finish(o_ref, l_ref, acc_ref, n_grp, n_rep, tq, dv)


def _attention(q, k, v, mask, *, n_grp, n_rep, dk, dv, scale, mode, tq, tk, k_tile=0, v_tile=0, skip=True):
    b, sq, _ = q.shape
    sk = k.shape[1]
    n_kc = sk // tk

    def kc_eff(j, c):
        if not skip:
            return c
        hi = (j * tq + tq - 1) // tk
        c = jnp.minimum(c, hi)
        if mode == "window":
            c = jnp.maximum(c, jnp.maximum(j * tq - WINDOW, 0) // tk)
        return c

    in_specs = [
        pl.BlockSpec((1, tq, n_grp * n_rep * dk), lambda i, j, c: (i, j, 0)),
        pl.BlockSpec((1, tk, n_grp * dk), lambda i, j, c: (i, kc_eff(j, c), k_tile)),
        pl.BlockSpec((1, tk, n_grp * dv), lambda i, j, c: (i, kc_eff(j, c), v_tile)),
    ]
    args = [q, k, v]
    mask_groups = 1
    if mode == "array":
        mb, mask_groups = mask.shape[0], mask.shape[1]
        in_specs.append(pl.BlockSpec((1, mask_groups, tq, tk),
                                     lambda i, j, c: (i if mb > 1 else 0, 0, j, kc_eff(j, c))))
        args.append(mask)
    rows = n_rep * tq
    kern = functools.partial(_attn_kernel, n_grp=n_grp, n_rep=n_rep, dk=dk, dv=dv, scale=scale, mode=mode,
                             tq=tq, tk=tk, window=WINDOW, mask_groups=mask_groups, skip=skip)
    return pl.pallas_call(
        kern,
        grid=(b, sq // tq, n_kc),
        in_specs=in_specs,
        out_specs=pl.BlockSpec((1, tq, n_grp * n_rep * dv), lambda i, j, c: (i, j, 0)),
        out_shape=jax.ShapeDtypeStruct((b, sq, n_grp * n_rep * dv), F32),
        scratch_shapes=[pltpu.VMEM((n_grp, rows, 1), F32), pltpu.VMEM((n_grp, rows, 1), F32),
                        pltpu.VMEM((n_grp, rows, dv), F32)],
        compiler_params=_params(("parallel", "parallel", "arbitrary")),
        name="attention_" + mode,
    )(*args)


def _paged_attn_kernel(pt_ref, q_ref, bias_ref, bias_new_ref, new_ref, *rest, n_per_step, n_grp, n_rep, mask_groups):
    pages = rest[:n_per_step]
    o_ref, m_ref, l_ref, acc_ref = rest[n_per_step:]
    c = pl.program_id(1)
    t = q_ref.shape[1]

    @pl.when(c == 0)
    def _():
        _attn_init(m_ref, l_ref, acc_ref)

    q = q_ref[0] * HEAD_DIM ** -0.5

    def k_of(page_ref, g):
        return page_ref[g * HEAD_DIM:(g + 1) * HEAD_DIM, :].astype(BF16)

    def v_of(page_ref, g):
        return page_ref[LANE + g * HEAD_DIM:LANE + (g + 1) * HEAD_DIM, :].astype(BF16)

    def update(page_refs, bias_of):
        n = len(page_refs)
        for g in range(n_grp):
            gm = g if mask_groups > 1 else 0
            qg = _stack_heads(q, g, n_rep, HEAD_DIM)
            s = jnp.concatenate([_dot(qg, k_of(page_refs[p], g)) + _tile_rows(bias_of(gm, p), n_rep)
                                 for p in range(n)], axis=1)

            def pv(pb, g=g):
                out = None
                for p in range(n):
                    term = _dot_nt(pb[:, p * PAGE_SIZE:(p + 1) * PAGE_SIZE], v_of(page_refs[p], g))
                    out = term if out is None else out + term
                return out

            _online_update(s, g, m_ref, l_ref, acc_ref, pv)

    update(pages, lambda gm, p: bias_ref[0, gm, p])

    @pl.when(c == pl.num_programs(1) - 1)
    def _():
        update([new_ref.at[0]], lambda gm, p: bias_new_ref[0, gm, 0])
        _attn_finish(o_ref, l_ref, acc_ref, n_grp, n_rep, t, HEAD_DIM)


def _paged_attention(q, bias, new_t, cache_t, pt_flat, n_pages, n_per_step, n_grp, n_rep):
    b, t, hd = q.shape
    mask_groups = bias.shape[1]
    rows = n_rep * t
    grid_spec = pltpu.PrefetchScalarGridSpec(
        num_scalar_prefetch=1,
        grid=(b, n_pages // n_per_step),
        in_specs=[
            pl.BlockSpec((1, t, hd), lambda i, c, pt: (i, 0, 0)),
            pl.BlockSpec((1, mask_groups, n_per_step, t, LANE), lambda i, c, pt: (i, 0, c, 0, 0)),
            pl.BlockSpec((1, mask_groups, 1, t, LANE), lambda i, c, pt: (i, 0, n_pages, 0, 0)),
            pl.BlockSpec((1, 2 * LANE, PAGE_SIZE), lambda i, c, pt: (i, 0, 0)),
        ] + _page_specs(n_per_step, n_pages, 2 * LANE, PAGE_SIZE),
        out_specs=pl.BlockSpec((1, t, hd), lambda i, c, pt: (i, 0, 0)),
        scratch_shapes=[pltpu.VMEM((n_grp, rows, 1), F32), pltpu.VMEM((n_grp, rows, 1), F32),
                        pltpu.VMEM((n_grp, rows, HEAD_DIM), F32)],
    )
    return pl.pallas_call(
        functools.partial(_paged_attn_kernel, n_per_step=n_per_step, n_grp=n_grp, n_rep=n_rep,
                          mask_groups=mask_groups),
        grid_spec=grid_spec,
        out_shape=jax.ShapeDtypeStruct((b, t, hd), F32),
        compiler_params=_params(("parallel", "arbitrary")),
        name="paged_attention",
    )(pt_flat, q, bias, bias, new_t, *([cache_t] * n_per_step))


def _odd_proj_kernel(*refs, with_kv):
    if with_kv:
        (x_ref, g_ref, w_ref, ncq_ref, nckv_ref, wqb_ref, gq_ref, c_ref, s1_ref, s2_ref, ones_ref,
         wuk_ref, gk_ref, wuv_ref, q_ref, lat_ref, k_ref, v_ref) = refs
    else:
        (x_ref, g_ref, w_ref, ncq_ref, nckv_ref, wqb_ref, gq_ref, c_ref, s1_ref, s2_ref, ones_ref,
         q_ref, lat_ref) = refs
    xn = _rms_rows(x_ref[...], g_ref[...]).astype(BF16)
    proj = _dot(xn, w_ref[...])
    c, s1, s2 = c_ref[...], s1_ref[...], s2_ref[...]
    ones = ones_ref[...]
    half = MLA_ROPE // 2

    def head_norm(x, gain):
        return x * lax.rsqrt(_group_mean_sq(x, ones) + NORM_EPS) * gain

    cqn = _rms_rows(proj[:, :Q_LORA], ncq_ref[...]).astype(BF16)
    qp = _dot(cqn, wqb_ref[...])
    c_lat = _rms_rows(proj[:, Q_LORA:Q_LORA + KV_LORA], nckv_ref[...])
    kr = _rope_lanes(proj[:, Q_LORA + KV_LORA:], c, s1, s2, half)
    lat_ref[:, :KV_LORA] = c_lat
    lat_ref[:, KV_LORA:] = kr[:, MLA_NOPE:MLA_QK]
    for h in range(MLA_HEADS):
        sl = slice(h * MLA_PAD, (h + 1) * MLA_PAD)
        q_ref[:, sl] = head_norm(_rope_lanes(qp[:, sl], c, s1, s2, half), gq_ref[:, sl]).astype(q_ref.dtype)
    if with_kv:
        cb = c_lat.astype(BF16)
        kn = _dot(cb, wuk_ref[...])
        for h in range(MLA_HEADS):
            sl = slice(h * MLA_PAD, (h + 1) * MLA_PAD)
            k_ref[:, sl] = head_norm(kn[:, sl] + kr, gk_ref[:, sl]).astype(k_ref.dtype)
        v_ref[...] = _dot(cb, wuv_ref[...]).astype(v_ref.dtype)


def _odd_project(x, g, w, ncq, nckv, wqb, gq, tabs, tab_period, ones, kv_weights, tm):
    m, d = x.shape
    c, s1, s2 = tabs
    row = lambda i: (i, 0)
    fixed = lambda i: (0, 0)
    tab = lambda i: (i % tab_period, 0)
    with_kv = kv_weights is not None
    args = [x, g.reshape(1, d), w, ncq, nckv, wqb, gq, c, s1, s2, ones]
    in_specs = [pl.BlockSpec((tm, d), row), pl.BlockSpec((1, d), fixed), pl.BlockSpec(w.shape, fixed),
                pl.BlockSpec(ncq.shape, fixed), pl.BlockSpec(nckv.shape, fixed), pl.BlockSpec(wqb.shape, fixed),
                pl.BlockSpec(gq.shape, fixed),
                pl.BlockSpec((tm, LANE), tab), pl.BlockSpec((tm, LANE), tab), pl.BlockSpec((tm, LANE), tab),
                pl.BlockSpec(ones.shape, fixed)]
    widths = [MLA_HEADS * MLA_PAD, KV_LORA + MLA_ROPE]
    if with_kv:
        args += list(kv_weights)
        in_specs += [pl.BlockSpec(a.shape, fixed) for a in kv_weights]
        widths += [MLA_HEADS * MLA_PAD, MLA_HEADS * MLA_V]
    dtypes = [BF16, F32, BF16, BF16] if with_kv else [F32, F32]
    return pl.pallas_call(
        functools.partial(_odd_proj_kernel, with_kv=with_kv),
        grid=(m // tm,),
        in_specs=in_specs,
        out_specs=[pl.BlockSpec((tm, wd), row) for wd in widths],
        out_shape=[jax.ShapeDtypeStruct((m, wd), dt) for wd, dt in zip(widths, dtypes)],
        compiler_params=_params(("parallel",)),
        name="odd_project",
    )(*args)


def _mla_paged_kernel(pt_ref, q_ref, gk_ref, wuk_ref, wuv_ref, new_ref, *rest, n_per_step):
    pages = rest[:n_per_step]
    o_ref, buf_ref, qbd_ref, qr_ref, m_ref, l_ref, acc_ref = rest[n_per_step:]
    c = pl.program_id(1)
    t = q_ref.shape[1]
    rows = MLA_HEADS * t
    d_nope = MLA_HEADS * MLA_NOPE
    scale = MLA_QK ** -0.5

    def head_diag(shape):
        return (lax.broadcasted_iota(jnp.int32, shape, 0) // t
                == lax.broadcasted_iota(jnp.int32, shape, 1) // MLA_NOPE)

    @pl.when(c == 0)
    def _():
        qg = q_ref[0] * gk_ref[...]
        nope = jnp.concatenate([qg[:, h * MLA_PAD:h * MLA_PAD + MLA_NOPE] for h in range(MLA_HEADS)], axis=1)
        diag = head_diag((rows, d_nope))
        qbd_ref[...] = jnp.where(diag, _tile_rows(nope, MLA_HEADS), 0.0).astype(BF16)
        qr_ref[...] = jnp.concatenate([qg[:, h * MLA_PAD + MLA_NOPE:h * MLA_PAD + MLA_QK]
                                       for h in range(MLA_HEADS)], axis=0).astype(BF16)
        m_ref[...] = jnp.full(m_ref.shape, NEG_BIG, F32)
        l_ref[...] = jnp.zeros_like(l_ref)
        acc_ref[...] = jnp.zeros_like(acc_ref)

    def process(lat_t, ok):
        n = lat_t.shape[1]
        cb = lat_t[:KV_LORA].astype(BF16)
        kr = lat_t[KV_LORA:]
        kn = _dot(wuk_ref[...], cb)
        s = _dot(qbd_ref[...], kn.astype(BF16)) + _dot(qr_ref[...], kr.astype(BF16))
        ss = (jnp.sum((kn * kn).reshape(MLA_HEADS, MLA_NOPE, n), axis=1)
              + jnp.sum(kr * kr, axis=0, keepdims=True))
        r = lax.rsqrt(ss * (1.0 / MLA_QK) + NORM_EPS) * scale
        s = s * jnp.broadcast_to(r[:, None, :], (MLA_HEADS, t, n)).reshape(rows, n)
        if ok is not None:
            s = jnp.where(ok, s, MASK_BIAS)
        m_prev = m_ref[...]
        m_new = jnp.maximum(m_prev, jnp.max(s, axis=-1, keepdims=True))
        p = jnp.exp(s - m_new)
        corr = jnp.exp(m_prev - m_new)
        l_ref[...] = l_ref[...] * corr + jnp.sum(p, axis=-1, keepdims=True)
        acc_ref[...] = acc_ref[...] * corr + _dot_nt(p.astype(BF16), cb)
        m_ref[...] = m_new

    for p in range(n_per_step):
        buf_ref[:, p * PAGE_SIZE:(p + 1) * PAGE_SIZE] = pages[p][...]
    process(buf_ref[...], None)

    @pl.when(c == pl.num_programs(1) - 1)
    def _():
        q_t = lax.broadcasted_iota(jnp.int32, (rows, PAGE_SIZE), 0) % t
        k_t = lax.broadcasted_iota(jnp.int32, (rows, PAGE_SIZE), 1)
        process(new_ref[0], k_t <= q_t)
        o_lat = (acc_ref[...] / l_ref[...]).astype(BF16)
        full = jnp.where(head_diag((rows, MLA_HEADS * MLA_V)), _dot(o_lat, wuv_ref[...]), 0.0)
        out = full[0:t]
        for h in range(1, MLA_HEADS):
            out = out + full[h * t:(h + 1) * t]
        o_ref[0] = out


def _mla_paged(q, gk, wuk_t, wuv, new_t, cache_t, pt_flat, n_pages, n_per_step):
    b, t, qd = q.shape
    rows = MLA_HEADS * t
    lat_w = KV_LORA + MLA_ROPE
    grid_spec = pltpu.PrefetchScalarGridSpec(
        num_scalar_prefetch=1,
        grid=(b, n_pages // n_per_step),
        in_specs=[
            pl.BlockSpec((1, t, qd), lambda i, c, pt: (i, 0, 0)),
            pl.BlockSpec(gk.shape, lambda i, c, pt: (0, 0)),
            pl.BlockSpec(wuk_t.shape, lambda i, c, pt: (0, 0)),
            pl.BlockSpec(wuv.shape, lambda i, c, pt: (0, 0)),
            pl.BlockSpec((1, lat_w, PAGE_SIZE), lambda i, c, pt: (i, 0, 0)),
        ] + _page_specs(n_per_step, n_pages, lat_w, PAGE_SIZE),
        out_specs=pl.BlockSpec((1, t, MLA_HEADS * MLA_V), lambda i, c, pt: (i, 0, 0)),
        scratch_shapes=[
            pltpu.VMEM((lat_w, n_per_step * PAGE_SIZE), F32),
            pltpu.VMEM((rows, MLA_HEADS * MLA_NOPE), BF16),
            pltpu.VMEM((rows, MLA_ROPE), BF16),
            pltpu.VMEM((rows, 1), F32), pltpu.VMEM((rows, 1), F32), pltpu.VMEM((rows, KV_LORA), F32),
        ],
    )
    return pl.pallas_call(
        functools.partial(_mla_paged_kernel, n_per_step=n_per_step),
        grid_spec=grid_spec,
        out_shape=jax.ShapeDtypeStruct((b, t, MLA_HEADS * MLA_V), F32),
        compiler_params=_params(("parallel", "arbitrary")),
        name="mla_paged",
    )(pt_flat, q, gk, wuk_t, wuv, new_t, *([cache_t] * n_per_step))


def _even_out_kernel(y_ref, oc_ref, os_ref, ow_ref, ob_ref, misc_ref, w_ref, o_ref):
    gate = misc_ref[...]
    tm = gate.shape[0]
    lane = lax.broadcasted_iota(jnp.int32, (tm, LANE), 1)

    def gate_pair(i, j):
        a = GATE_LANE0 + 3 * (2 * i) + j
        b = GATE_LANE0 + 3 * (2 * i + 1) + j
        return jnp.where(lane < HEAD_DIM, jnp.broadcast_to(gate[:, a:a + 1], (tm, LANE)),
                         jnp.broadcast_to(gate[:, b:b + 1], (tm, LANE)))

    n_a = oc_ref.shape[1] // LANE
    acc = y_ref[...]
    for i in range(n_a):
        sl = slice(i * LANE, (i + 1) * LANE)
        oa = gate_pair(i, 0) * oc_ref[:, sl] + gate_pair(i, 1) * os_ref[:, sl] + gate_pair(i, 2) * ow_ref[:, sl]
        acc = acc + _dot(oa.astype(BF16), w_ref[sl, :])
    acc = acc + _dot(ob_ref[...].astype(BF16), w_ref[n_a * LANE:, :])
    o_ref[...] = acc


def _even_out(y, oc, os_, ow, ob, misc, w, tm):
    m, d = y.shape
    row = lambda i: (i, 0)
    return pl.pallas_call(
        _even_out_kernel,
        grid=(m // tm,),
        in_specs=[pl.BlockSpec((tm, d), row)] + [pl.BlockSpec((tm, 512), row)] * 4
                 + [pl.BlockSpec((tm, LANE), row), pl.BlockSpec(w.shape, lambda i: (0, 0))],
        out_specs=pl.BlockSpec((tm, d), row),
        out_shape=jax.ShapeDtypeStruct((m, d), F32),
        compiler_params=_params(("parallel",)),
        name="even_out",
    )(y, oc, os_, ow, ob, misc, w)


def _odd_out_kernel(y_ref, o_ref_in, w_ref, o_ref):
    o_ref[...] = y_ref[...] + _dot(o_ref_in[...].astype(BF16), w_ref[...])


def _odd_out(y, o, w, tm):
    m, d = y.shape
    row = lambda i: (i, 0)
    return pl.pallas_call(
        _odd_out_kernel,
        grid=(m // tm,),
        in_specs=[pl.BlockSpec((tm, d), row), pl.BlockSpec((tm, o.shape[1]), row),
                  pl.BlockSpec(w.shape, lambda i: (0, 0))],
        out_specs=pl.BlockSpec((tm, d), row),
        out_shape=jax.ShapeDtypeStruct((m, d), F32),
        compiler_params=_params(("parallel",)),
        name="odd_out",
    )(y, o, w)


def _row_tile(m, cap=1024):
    t = cap
    while m % t:
        t //= 2
    return t


def _rope_tables(pos, half, seg_starts):
    inv = ROPE_THETA ** (-jnp.arange(half, dtype=F32) / half)
    ang = pos.astype(F32)[:, None] * inv[None, :]
    cos, sin = jnp.cos(ang), jnp.sin(ang)
    n = pos.shape[0]
    c = jnp.ones((n, LANE), F32)
    s1 = jnp.zeros((n, LANE), F32)
    s2 = jnp.zeros((n, LANE), F32)
    for st in seg_starts:
        c = c.at[:, st:st + half].set(cos).at[:, st + half:st + 2 * half].set(cos)
        s1 = s1.at[:, st:st + half].set(-sin)
        s2 = s2.at[:, st + half:st + 2 * half].set(sin)
    return c, s1, s2


def _tile_tables(tabs, reps):
    return tuple(jnp.tile(a, (reps, 1)) for a in tabs)


def _block_mean_matrix(group):
    i = np.arange(LANE)
    return jnp.asarray((i[:, None] // group == i[None, :] // group) / group, BF16)


def _even_weights(w_in, nsa_g, dsa_g):
    d = w_in.shape[0]
    n_q = NSA_KV_HEADS * NSA_GROUP * HEAD_DIM
    n_kv = 2 * NSA_KV_HEADS * HEAD_DIM
    splits = np.cumsum([n_q, 3 * n_kv, 3 * NSA_KV_HEADS * NSA_GROUP, n_q, n_kv, IDX_HEADS * IDX_DIM, IDX_HEADS])
    q_a, kv_a, gate, q_b, kv_b, qi, wi, ki = jnp.split(w_in, splits.tolist(), axis=1)
    pad = jnp.zeros((d, LANE - ki.shape[1] - gate.shape[1] - wi.shape[1]), w_in.dtype)
    w = jnp.concatenate([q_a, kv_a, q_b, kv_b, qi, ki, gate, wi, pad], axis=1).astype(BF16)
    one = jnp.ones((LANE,), F32)
    pair = lambda g: jnp.tile(g, 2)
    gain = jnp.concatenate(
        [pair(nsa_g[0])] * 4
        + [pair(nsa_g[1]), one, pair(nsa_g[2]), one, pair(nsa_g[3]), one]
        + [pair(dsa_g[0])] * 4 + [pair(dsa_g[1]), one] + [one] * 3).reshape(1, EVEN_TILES * LANE)
    return w, gain


def _compress_weights(pe, w1, w2):
    pe_t = jnp.concatenate([jnp.tile(pe[k], (1, NSA_KV_HEADS)) for k in range(2)], axis=1)
    hid = w1.shape[2]
    w1r = w1.reshape(2, CMP_BLOCK, HEAD_DIM, hid)
    z1 = jnp.zeros_like(w1r)
    w1bd = jnp.concatenate([jnp.concatenate([w1r, z1], axis=3), jnp.concatenate([z1, w1r], axis=3)], axis=2)
    z2 = jnp.zeros_like(w2)
    w2bd = jnp.concatenate([jnp.concatenate([w2, z2], axis=2), jnp.concatenate([z2, w2], axis=2)], axis=1)
    return pe_t, w1bd.astype(BF16), w2bd.astype(BF16)


def _pair_sum_matrix(nc, ns_pad):
    ratio = SEL_BLOCK // CMP_BLOCK
    return jnp.asarray(np.arange(nc)[:, None] // ratio == np.arange(ns_pad)[None, :], F32)


def _pad_heads(w, width):
    r, h, _ = w.shape
    return jnp.pad(w, ((0, 0), (0, 0), (0, LANE - width))).reshape(r, h * LANE)


def _pad_rows(x, n):
    return jnp.pad(x, ((0, 0), (0, n - x.shape[1]), (0, 0)))


def kernel(x_prompt, x_sample, cache_nsa_cmp_kv, cache_nsa_slc_kv, state_nsa_win_kv, cache_dsa_kv, cache_dsa_idx_k, cache_mla_latent, page_table, norm_g, ffn_w_gu, ffn_w_down, even_w_in, nsa_qk_g, nsa_cmp_pe, nsa_cmp_w1, nsa_cmp_w2, dsa_qk_g, even_w_out, odd_w_in, mla_norm_cq, mla_norm_ckv, mla_w_qb, mla_w_uk, mla_w_uv, mla_qk_g, odd_w_out):
    bp, s, d = x_prompt.shape
    bs, t, _ = x_sample.shape
    n_pages = page_table.shape[1]
    past = n_pages * PAGE_SIZE
    depth = norm_g.shape[0]
    n_pool = cache_nsa_cmp_kv.shape[1]
    assert s % CMP_BLOCK == 0 and s % LANE == 0 and t < CMP_BLOCK and t % SUBLANE == 0
    kv_w = 2 * NSA_KV_HEADS * HEAD_DIM
    lat_w = KV_LORA + MLA_ROPE
    yp = x_prompt.reshape(bp * s, d)
    ys = x_sample.reshape(bs * t, d)
    pt_flat = page_table.reshape(-1)
    tm_p = _row_tile(s)
    tm_s = _row_tile(bs * t, 512)
    tm_proj = _row_tile(s, 512)
    tm_odd = _row_tile(s, 256)
    pos_p = jnp.arange(s)
    pos_s = past + jnp.arange(t)
    tq = min(128, s)
    tq_attn = min(128, s)
    tk = min(1024, s)
    pages_per_step = min(64, n_pages)
    mla_pages_per_step = min(32, n_pages)

    nsa_tabs_p = _rope_tables(pos_p, ROT_DIM // 2, (0, HEAD_DIM))
    nsa_tabs_s = _tile_tables(_rope_tables(pos_s, ROT_DIM // 2, (0, HEAD_DIM)), tm_s // t)
    mla_tabs_p = _rope_tables(pos_p, MLA_ROPE // 2, (MLA_NOPE,))
    mla_tabs_s = _tile_tables(_rope_tables(pos_s, MLA_ROPE // 2, (MLA_NOPE,)), tm_s // t)
    bd64 = _block_mean_matrix(HEAD_DIM)
    ones96 = jnp.full((LANE, LANE), 1.0 / MLA_QK, BF16)

    def ffn(y, layer, i, gi, tm):
        return _ffn_half(y, norm_g[layer, gi], ffn_w_gu[layer, i].astype(BF16), ffn_w_down[layer, i].astype(BF16), tm)

    even_p = [[] for _ in range(5)]
    even_s = [[] for _ in range(5)]
    mla_p, mla_s = [], []
    for layer in range(depth):
        li = layer // 2
        yp = ffn(yp, layer, 0, 0, tm_p)
        ys = ffn(ys, layer, 0, 0, tm_s)
        if layer % 2 == 0:
            w_in, gain = _even_weights(even_w_in[li], nsa_qk_g[li], dsa_qk_g[li])
            pe_t, w1bd, w2bd = _compress_weights(nsa_cmp_pe[li], nsa_cmp_w1[li], nsa_cmp_w2[li])
            w_out = even_w_out[li].astype(BF16)
            g1 = norm_g[layer, 1]

            qa, kcmp, kslc, kwin, qb, kvb, qi, misc, ki = _even_project(
                yp, g1, w_in, gain, nsa_tabs_p, s // tm_proj, bd64, tm_proj)
            r3 = lambda a: a.reshape(bp, s, a.shape[1])
            nc = s // CMP_BLOCK
            kvc = _compress_dense(kcmp, pe_t, w1bd, w2bd, _row_tile(bp * s, 4096)).reshape(bp, nc, kv_w)
            ns_pad = -(-(s // SEL_BLOCK) // LANE) * LANE
            oc, mask_slc = _nsa_cmp_select(r3(qa), kvc, _pair_sum_matrix(nc, ns_pad), tq, 0, s, s, False)
            attn = functools.partial(_attention, n_grp=NSA_KV_HEADS, n_rep=NSA_GROUP, dk=HEAD_DIM, dv=HEAD_DIM,
                                     scale=HEAD_DIM ** -0.5, tq=tq_attn, tk=tk, k_tile=0, v_tile=1)
            o_slc = attn(r3(qa), r3(kslc), r3(kslc), mask_slc, mode="array")
            o_win = attn(r3(qa), r3(kwin), r3(kwin), None, mode="window")
            ki_t = jnp.swapaxes(ki.reshape(bp, s, IDX_DIM), 1, 2)
            mask_dsa = _dsa_index_dense(r3(qi), r3(misc), ki_t, min(256, s), min(DSA_TOPK, s // 4))
            o_dsa = attn(r3(qb), r3(kvb), r3(kvb), mask_dsa, mode="array")
            f2 = lambda a: a.reshape(bp * s, a.shape[2])
            yp = _even_out(yp, f2(oc), f2(o_slc), f2(o_win), f2(o_dsa), misc, w_out, tm_proj)
            rows6 = lambda a, n: a.reshape(n, -1, 2, NSA_KV_HEADS, HEAD_DIM)
            w_keep = min(WINDOW, s)
            for lst, a in zip(even_p, (rows6(kcmp, bp), rows6(kslc, bp), rows6(kwin, bp)[:, s - w_keep:],
                                       rows6(kvb, bp), ki.reshape(bp, s, IDX_DIM))):
                lst.append(a)

            qa, kcmp, kslc, kwin, qb, kvb, qi, misc, ki = _even_project(
                ys, g1, w_in, gain, nsa_tabs_s, 1, bd64, tm_s)
            r3 = lambda a: a.reshape(bs, t, a.shape[1])
            new_page = lambda a: jnp.swapaxes(_pad_rows(r3(a), PAGE_SIZE), 1, 2)
            cache2 = lambda c: jnp.moveaxis(c[li].reshape(n_pool, PAGE_SIZE, -1), 1, 2)
            nc = n_pages * (PAGE_SIZE // CMP_BLOCK)
            n_keys = past + t
            lk = (n_pages + 1) * PAGE_SIZE
            kvc = _compress_paged(cache2(cache_nsa_cmp_kv), pt_flat, bs, n_pages, pe_t, w1bd, w2bd,
                                  min(32, n_pages)).reshape(bs, nc, kv_w)
            ns_pad = -(-(lk // SEL_BLOCK) // LANE) * LANE
            oc, mask_slc = _nsa_cmp_select(r3(qa), kvc, _pair_sum_matrix(nc, ns_pad), t, past, n_keys, lk, True)
            paged = functools.partial(_paged_attention, pt_flat=pt_flat, n_pages=n_pages,
                                      n_per_step=pages_per_step, n_grp=NSA_KV_HEADS, n_rep=NSA_GROUP)
            o_slc = paged(r3(qa), mask_slc, new_page(kslc), cache2(cache_nsa_slc_kv))
            kv_win = jnp.concatenate([state_nsa_win_kv[li].reshape(bs, -1, kv_w), r3(kwin)], axis=1)
            wb = kv_win.shape[1] - t
            lw = -(-(wb + t) // LANE) * LANE
            dpos = (past + np.arange(t))[:, None] - (past - wb + np.arange(lw))[None, :]
            ok_w = (dpos >= 0) & (dpos <= WINDOW) & (np.arange(lw)[None, :] < wb + t) & ((past - wb + np.arange(lw))[None, :] >= 0)
            mask_win = jnp.asarray(np.where(ok_w, 0.0, MASK_BIAS)[None, None], BF16)
            kv_win_pad = _pad_rows(kv_win, lw)
            o_win = _attention(r3(qa), kv_win_pad, kv_win_pad, mask_win, n_grp=NSA_KV_HEADS, n_rep=NSA_GROUP,
                               dk=HEAD_DIM, dv=HEAD_DIM, scale=HEAD_DIM ** -0.5, mode="array", tq=t, tk=lw,
                               k_tile=0, v_tile=1, skip=False)
            mask_dsa = _dsa_index_paged(r3(qi), r3(misc), new_page(ki), cache2(cache_dsa_idx_k), pt_flat,
                                        n_pages, pages_per_step, min(DSA_TOPK, n_keys // 4))
            o_dsa = paged(r3(qb), mask_dsa, new_page(kvb), cache2(cache_dsa_kv))
            f2 = lambda a: a.reshape(bs * t, a.shape[2])
            ys = _even_out(ys, f2(oc), f2(o_slc), f2(o_win), f2(o_dsa), misc, w_out, tm_s)
            w_keep = min(WINDOW, wb + t)
            for lst, a in zip(even_s, (rows6(kcmp, bs), rows6(kslc, bs), rows6(kv_win[:, wb + t - w_keep:], bs),
                                       rows6(kvb, bs), ki.reshape(bs, t, IDX_DIM))):
                lst.append(a)
        else:
            cq, ckv, kr = jnp.split(odd_w_in[li], [Q_LORA, Q_LORA + KV_LORA], axis=1)
            z = lambda n: jnp.zeros((d, n), odd_w_in.dtype)
            w_in = jnp.concatenate([cq, ckv, z(MLA_NOPE), kr, z(LANE - MLA_QK)], axis=1).astype(BF16)
            wqb = _pad_heads(mla_w_qb[li], MLA_QK).astype(BF16)
            wuk_pad = _pad_heads(mla_w_uk[li], MLA_NOPE).astype(BF16)
            wuk_t = mla_w_uk[li].reshape(KV_LORA, MLA_HEADS * MLA_NOPE).T.astype(BF16)
            wuv = mla_w_uv[li].reshape(KV_LORA, MLA_HEADS * MLA_V).astype(BF16)
            pad_gain = lambda g: jnp.tile(jnp.pad(g, (0, LANE - MLA_QK)), MLA_HEADS).reshape(1, MLA_HEADS * LANE)
            gq, gk = pad_gain(mla_qk_g[li, 0]), pad_gain(mla_qk_g[li, 1])
            ncq = mla_norm_cq[li].reshape(1, Q_LORA)
            nckv = mla_norm_ckv[li].reshape(1, KV_LORA)
            w_out = odd_w_out[li].astype(BF16)
            g1 = norm_g[layer, 1]

            q, lat, k, v = _odd_project(yp, g1, w_in, ncq, nckv, wqb, gq, mla_tabs_p, s // tm_odd, ones96,
                                        (wuk_pad, gk, wuv), tm_odd)
            r3 = lambda a: a.reshape(bp, s, a.shape[1])
            o = _attention(r3(q), r3(k), r3(v), None, n_grp=MLA_HEADS, n_rep=1, dk=MLA_PAD, dv=MLA_V,
                           scale=MLA_QK ** -0.5, mode="causal", tq=min(256, s), tk=tk)
            yp = _odd_out(yp, o.reshape(bp * s, -1), w_out, tm_p)
            mla_p.append(lat.reshape(bp, s, lat_w))

            q, lat = _odd_project(ys, g1, w_in, ncq, nckv, wqb, gq, mla_tabs_s, 1, ones96, None, tm_s)
            lat3 = lat.reshape(bs, t, lat_w)
            o = _mla_paged(q.reshape(bs, t, -1), gk, wuk_t, wuv, jnp.swapaxes(_pad_rows(lat3, PAGE_SIZE), 1, 2),
                           jnp.swapaxes(cache_mla_latent[li], 1, 2), pt_flat, n_pages, mla_pages_per_step)
            ys = _odd_out(ys, o.reshape(bs * t, -1), w_out, tm_s)
            mla_s.append(lat3)
        yp = ffn(yp, layer, 1, 2, tm_p)
        ys = ffn(ys, layer, 1, 2, tm_s)

    stack = lambda lst: jnp.stack(lst)
    cmp_p, slc_p, win_p, dsa_p, idx_p = [stack(a) for a in even_p]
    cmp_s, slc_s, win_s, dsa_s, idx_s = [stack(a) for a in even_s]
    return (yp.reshape(bp, s, d), ys.reshape(bs, t, d), cmp_p, cmp_s, slc_p, slc_s, win_p, win_s,
            dsa_p, dsa_s, idx_p, idx_s, stack(mla_p), stack(mla_s))
```
